```python
import math
import jax, jax.numpy as jnp
from jax import lax
import numpy as np

D_MODEL = 1024
BATCH = 8
SEQ = 4096
DEPTH = 2

MEM_LEN = 256
D_MIX = D_MODEL
ATT_HEADS = 8
ATT_KV_HEADS = 2
ATT_HEAD_DIM = D_MIX // (2 * ATT_HEADS)
WINDOW = 128
ATT_BLOCK = 128
GDN_HEADS = 4
GDN_HEAD_DIM = D_MIX // (2 * GDN_HEADS)
CONV_K = 4
GDN_CHUNK = 64
X_HEADS = 4
X_HEAD_DIM = D_MODEL // X_HEADS
D_FF = 4 * D_MODEL
ALPHA = (2 * DEPTH) ** 0.25
BETA_INIT = (8 * DEPTH) ** -0.25
LN_EPS = 1e-5
RMS_EPS = 1e-6
NEG_INF = -1e30

ATT_Q = ATT_HEADS * ATT_HEAD_DIM
ATT_KV = ATT_KV_HEADS * ATT_HEAD_DIM
GDN_W = GDN_HEADS * GDN_HEAD_DIM
IN_SPLITS = (ATT_Q, ATT_KV, ATT_KV, 3 * GDN_W, GDN_HEADS, GDN_HEADS, GDN_W)
D_IN = ATT_Q + 2 * ATT_KV + 4 * GDN_W + 2 * GDN_HEADS

kernel_name = "hybrid_swa_sink_gdn_deepnorm"


def _split_points(sizes):
    pts, acc = [], 0
    for s in sizes[:-1]:
        acc += s
        pts.append(acc)
    return pts


def alibi_slopes(n):
    return jnp.exp2(-8.0 * (jnp.arange(n, dtype=jnp.float32) + 1.0) / n)


def layer_norm(x, g, b):
    xf = x.astype(jnp.float32)
    mu = xf.mean(-1, keepdims=True)
    var = jnp.square(xf - mu).mean(-1, keepdims=True)
    y = (xf - mu) * lax.rsqrt(var + LN_EPS) * g.astype(jnp.float32) + b.astype(jnp.float32)
    return y.astype(x.dtype)


def l2norm(t):
    return t * lax.rsqrt(jnp.sum(t * t, axis=-1, keepdims=True) + RMS_EPS)


def sliding_window_attention(q, k, v, sinks):
    B, S = q.shape[0], q.shape[1]
    nb = S // ATT_BLOCK
    G = ATT_HEADS // ATT_KV_HEADS
    qb = q.reshape(B, nb, ATT_BLOCK, ATT_KV_HEADS, G, ATT_HEAD_DIM)

    def band(t):
        tp = jnp.pad(t, ((0, 0), (ATT_BLOCK, 0), (0, 0), (0, 0)))
        tp = tp.reshape(B, nb + 1, ATT_BLOCK, ATT_KV_HEADS, ATT_HEAD_DIM)
        return jnp.concatenate([tp[:, :-1], tp[:, 1:]], axis=2)

    kb, vb = band(k), band(v)
    s = jnp.einsum('bnqhgd,bnkhd->bnhgqk', qb, kb).astype(jnp.float32) * (ATT_HEAD_DIM ** -0.5)
    qi = jnp.arange(ATT_BLOCK)[:, None]
    kj = jnp.arange(2 * ATT_BLOCK)[None, :]
    dist = qi + ATT_BLOCK - kj
    key_pos = jnp.arange(nb)[:, None, None] * ATT_BLOCK + kj[None] - ATT_BLOCK
    valid = (dist >= 0) & (dist < WINDOW) & (key_pos >= 0)
    slopes = alibi_slopes(ATT_HEADS).reshape(ATT_KV_HEADS, G)
    bias = -slopes[:, :, None, None] * dist.astype(jnp.float32)
    s = jnp.where(valid[None, :, None, None], s + bias, NEG_INF)
    sink = jnp.broadcast_to(sinks.astype(jnp.float32).reshape(ATT_KV_HEADS, G, 1, 1),
                            s.shape[:-1] + (1,))
    p = jax.nn.softmax(jnp.concatenate([s, sink], axis=-1), axis=-1)[..., :-1]
    o = jnp.einsum('bnhgqk,bnkhd->bnqhgd', p.astype(v.dtype), vb)
    return o.reshape(B, S, ATT_Q)


def causal_depthwise_conv(x, w):
    C = x.shape[-1]
    return lax.conv_general_dilated(
        x, w[:, None, :].astype(x.dtype), window_strides=(1,), padding=[(CONV_K - 1, 0)],
        dimension_numbers=('NWC', 'WIO', 'NWC'), feature_group_count=C)


def gated_delta_rule(q, k, v, g, beta):
    B, S, H, Dk = q.shape
    Dv = v.shape[-1]
    C = GDN_CHUNK
    n = S // C

    def chunks(t):
        t = t.reshape((B, n, C, H) + t.shape[3:])
        return jnp.moveaxis(t, 3, 1)

    qc, kc, vc = chunks(q * (Dk ** -0.5)), chunks(k), chunks(v)
    gc, bc = chunks(g), chunks(beta)
    decay = jnp.cumsum(gc, axis=-1)
    causal = jnp.tril(jnp.ones((C, C), dtype=bool))
    strict = jnp.tril(jnp.ones((C, C), dtype=bool), -1)
    diff = decay[..., :, None] - decay[..., None, :]
    lmask = jnp.where(causal, jnp.exp(jnp.where(causal, diff, 0.0)), 0.0)
    kbeta = kc * bc[..., None]
    A = jnp.where(strict, jnp.einsum('bhnid,bhnjd->bhnij', kbeta, kc) * lmask, 0.0)
    eye = jnp.eye(C, dtype=jnp.float32)
    T = lax.linalg.triangular_solve(eye + A, jnp.broadcast_to(eye, A.shape),
                                    left_side=True, lower=True)
    u = jnp.einsum('bhnij,bhnjd->bhnid', T, vc * bc[..., None])
    w = jnp.einsum('bhnij,bhnjd->bhnid', T, kbeta * jnp.exp(decay)[..., None])
    qd = qc * jnp.exp(decay)[..., None]
    kd = kc * jnp.exp(decay[..., -1:] - decay)[..., None]
    intra = jnp.where(causal, jnp.einsum('bhnid,bhnjd->bhnij', qc, kc) * lmask, 0.0)
    cd = jnp.exp(decay[..., -1])

    def step(state, inp):
        qd_i, kd_i, u_i, w_i, a_i, cd_i = inp
        v_new = u_i - jnp.einsum('bhik,bhkv->bhiv', w_i, state)
        o_i = jnp.einsum('bhik,bhkv->bhiv', qd_i, state) + jnp.einsum('bhij,bhjv->bhiv', a_i, v_new)
        state = state * cd_i[..., None, None] + jnp.einsum('bhik,bhiv->bhkv', kd_i, v_new)
        return state, o_i

    xs = (jnp.moveaxis(qd, 2, 0), jnp.moveaxis(kd, 2, 0), jnp.moveaxis(u, 2, 0),
          jnp.moveaxis(w, 2, 0), jnp.moveaxis(intra, 2, 0), jnp.moveaxis(cd, 2, 0))
    state0 = jnp.zeros((B, H, Dk, Dv), jnp.float32)
    _, o = lax.scan(step, state0, xs)
    return jnp.moveaxis(o, 0, 2).transpose(0, 2, 3, 1, 4).reshape(B, S, H, Dv)


def hybrid_mixer(x, w_in, conv_w, sinks, a_log, dt_bias, norm_g, w_out):
    B, S, _ = x.shape
    h = x @ w_in
    aq, ak, av, gqkv, ga, gb, gz = jnp.split(h, _split_points(IN_SPLITS), axis=-1)
    att = sliding_window_attention(aq.reshape(B, S, ATT_HEADS, ATT_HEAD_DIM),
                                   ak.reshape(B, S, ATT_KV_HEADS, ATT_HEAD_DIM),
                                   av.reshape(B, S, ATT_KV_HEADS, ATT_HEAD_DIM), sinks)
    qkv = jax.nn.silu(causal_depthwise_conv(gqkv, conv_w)).astype(jnp.float32)
    q, k, v = jnp.split(qkv, 3, axis=-1)
    q = l2norm(q.reshape(B, S, GDN_HEADS, GDN_HEAD_DIM))
    k = l2norm(k.reshape(B, S, GDN_HEADS, GDN_HEAD_DIM))
    v = v.reshape(B, S, GDN_HEADS, GDN_HEAD_DIM)
    beta = jax.nn.sigmoid(gb.astype(jnp.float32))
    g = -jnp.exp(a_log.astype(jnp.float32)) * jax.nn.softplus(ga.astype(jnp.float32) + dt_bias.astype(jnp.float32))
    o = gated_delta_rule(q, k, v, g, beta)
    z = gz.astype(jnp.float32).reshape(B, S, GDN_HEADS, GDN_HEAD_DIM)
    o = (o * lax.rsqrt(jnp.mean(o * o, axis=-1, keepdims=True) + RMS_EPS)
         * norm_g.astype(jnp.float32) * jax.nn.silu(z))
    mix = jnp.concatenate([att, o.reshape(B, S, GDN_W).astype(x.dtype)], axis=-1)
    return mix @ w_out


def memory_cross_attention(x, mem, wq, wk, wv, wo):
    B, S, _ = x.shape
    M = mem.shape[1]
    q = (x @ wq).reshape(B, S, X_HEADS, X_HEAD_DIM)
    k = (mem @ wk).reshape(B, M, X_HEADS, X_HEAD_DIM)
    v = (mem @ wv).reshape(B, M, X_HEADS, X_HEAD_DIM)
    s = jnp.einsum('bshd,bmhd->bhsm', q, k).astype(jnp.float32) * (X_HEAD_DIM ** -0.5)
    p = jax.nn.softmax(s, axis=-1).astype(v.dtype)
    o = jnp.einsum('bhsm,bmhd->bshd', p, v).reshape(B, S, D_MODEL)
    return o @ wo


def squared_relu_mlp(x, w1, w2):
    return jnp.square(jax.nn.relu(x @ w1)) @ w2


def setup_inputs(seed: int = 0) -> dict:
    key = jax.random.key(seed)
    ks = jax.random.split(key, 20)
    nrm = jax.random.normal
    f32 = jnp.float32
    x = nrm(ks[0], (BATCH, SEQ, D_MODEL), f32)
    mem = nrm(ks[1], (BATCH, MEM_LEN, D_MODEL), f32)
    w_in = nrm(ks[2], (DEPTH, D_MODEL, D_IN), f32) * D_MODEL ** -0.5
    conv_w = nrm(ks[3], (DEPTH, CONV_K, 3 * GDN_W), f32) * CONV_K ** -0.5
    attn_sinks = nrm(ks[4], (DEPTH, ATT_HEADS), f32) * 0.5
    a_log = jnp.log(jax.random.uniform(ks[5], (DEPTH, GDN_HEADS), f32, 1.0, 16.0))
    dt = jnp.exp(jax.random.uniform(ks[6], (DEPTH, GDN_HEADS), f32, math.log(1e-3), math.log(1e-1)))
    dt_bias = dt + jnp.log(-jnp.expm1(-dt))
    gdn_norm_g = 1.0 + 0.02 * nrm(ks[7], (DEPTH, GDN_HEAD_DIM), f32)
    w_mix_out = nrm(ks[8], (DEPTH, D_MIX, D_MODEL), f32) * (D_MIX ** -0.5) * BETA_INIT
    wq_mem = nrm(ks[9], (DEPTH, D_MODEL, D_MODEL), f32) * D_MODEL ** -0.5
    wk_mem = nrm(ks[10], (DEPTH, D_MODEL, D_MODEL), f32) * D_MODEL ** -0.5
    wv_mem = nrm(ks[11], (DEPTH, D_MODEL, D_MODEL), f32) * D_MODEL ** -0.5
    wo_mem = nrm(ks[12], (DEPTH, D_MODEL, D_MODEL), f32) * (D_MODEL ** -0.5) * BETA_INIT
    w_ff1 = nrm(ks[13], (DEPTH, D_MODEL, D_FF), f32) * D_MODEL ** -0.5
    w_ff2 = nrm(ks[14], (DEPTH, D_FF, D_MODEL), f32) * (D_FF ** -0.5) * BETA_INIT
    ln_g = 1.0 + 0.02 * nrm(ks[15], (DEPTH, 3, D_MODEL), f32)
    ln_b = 0.02 * nrm(ks[16], (DEPTH, 3, D_MODEL), f32)
    return {"x": x, "mem": mem, "w_in": w_in, "conv_w": conv_w, "attn_sinks": attn_sinks,
            "a_log": a_log, "dt_bias": dt_bias, "gdn_norm_g": gdn_norm_g, "w_mix_out": w_mix_out,
            "wq_mem": wq_mem, "wk_mem": wk_mem, "wv_mem": wv_mem, "wo_mem": wo_mem,
            "w_ff1": w_ff1, "w_ff2": w_ff2, "ln_g": ln_g, "ln_b": ln_b}


def reference(x, mem, w_in, conv_w, attn_sinks, a_log, dt_bias, gdn_norm_g, w_mix_out,
              wq_mem, wk_mem, wv_mem, wo_mem, w_ff1, w_ff2, ln_g, ln_b):
    for l in range(DEPTH):
        y = hybrid_mixer(x, w_in[l], conv_w[l], attn_sinks[l], a_log[l], dt_bias[l],
                         gdn_norm_g[l], w_mix_out[l])
        x = layer_norm(ALPHA * x + y, ln_g[l, 0], ln_b[l, 0])
        y = memory_cross_attention(x, mem, wq_mem[l], wk_mem[l], wv_mem[l], wo_mem[l])
        x = layer_norm(ALPHA * x + y, ln_g[l, 1], ln_b[l, 1])
        y = squared_relu_mlp(x, w_ff1[l], w_ff2[l])
        x = layer_norm(ALPHA * x + y, ln_g[l, 2], ln_b[l, 2])
    return x
```

```python
import functools

import jax
import jax.numpy as jnp
from jax import lax
from jax.experimental import pallas as pl
from jax.experimental.pallas import tpu as pltpu

F32 = jnp.float32
BF16 = jnp.bfloat16

D_MODEL = 1024
DEPTH = 2
ATT_HEADS = 8
ATT_KV_HEADS = 2
ATT_HEAD_DIM = 64
ATT_GROUP = ATT_HEADS // ATT_KV_HEADS
WINDOW = 128
ATT_Q = ATT_HEADS * ATT_HEAD_DIM
ATT_KV = ATT_KV_HEADS * ATT_HEAD_DIM
GDN_HEADS = 4
GDN_HEAD_DIM = 128
GDN_W = GDN_HEADS * GDN_HEAD_DIM
CONV_K = 4
X_HEADS = 4
X_HEAD_DIM = D_MODEL // X_HEADS
D_FF = 4 * D_MODEL
ALPHA = (2 * DEPTH) ** 0.25
LN_EPS = 1e-5
RMS_EPS = 1e-6
NEG_INF = -1e30
ATT_SCALE = ATT_HEAD_DIM ** -0.5
X_SCALE = X_HEAD_DIM ** -0.5
ALIBI_SLOPES = tuple(2.0 ** (-8.0 * (h + 1) / ATT_HEADS) for h in range(ATT_HEADS))

LANES = 128
SUBLANES = 8
ROW_TILE = 512
ATT_ROWS = 512
GDN_CHUNK = 128
FF_CHUNK = 512
VMEM_LIMIT = 56 * 1024 * 1024

_C_AQ = (0, ATT_Q)
_C_AKV = (_C_AQ[1], _C_AQ[1] + 4 * LANES)
_C_GQKV = (_C_AKV[1], _C_AKV[1] + 3 * GDN_W)
_C_GZ = (_C_GQKV[1], _C_GQKV[1] + GDN_W)
_C_GAB = (_C_GZ[1], _C_GZ[1] + LANES)
N_IN = _C_GAB[1]


def _cparams(*sem):
    return pltpu.CompilerParams(dimension_semantics=sem, vmem_limit_bytes=VMEM_LIMIT)


def _resident(shape):
    nd = len(shape)
    return pl.BlockSpec(shape, lambda *_: (0,) * nd, pipeline_mode=pl.Buffered(1))


def _dot(a, b):
    return jnp.dot(a, b, preferred_element_type=F32)


def _dot_nt(a, b):
    return lax.dot_general(a, b, (((1,), (1,)), ((), ())), preferred_element_type=F32)


def _layer_norm(xf, g, b):
    mu = jnp.mean(xf, axis=-1, keepdims=True)
    d = xf - mu
    var = jnp.mean(d * d, axis=-1, keepdims=True)
    return d * lax.rsqrt(var + LN_EPS) * g + b


def _silu(x):
    return x * jax.nn.sigmoid(x)


def _inproj_body(x_ref, w_ref, aq_ref, akv_ref, gqkv_ref, gz_ref, gab_ref):
    xb = x_ref[...].astype(BF16)

    def mm(lo, hi):
        return _dot(xb, w_ref[:, lo:hi])

    aq_ref[...] = mm(*_C_AQ).astype(BF16)
    akv_ref[...] = mm(*_C_AKV).astype(BF16)
    for c in range(3):
        lo = _C_GQKV[0] + c * GDN_W
        gqkv_ref[:, c * GDN_W:(c + 1) * GDN_W] = mm(lo, lo + GDN_W)
    gz_ref[...] = mm(*_C_GZ)
    gab_ref[...] = mm(*_C_GAB)


def _inproj(x, w_all):
    t = x.shape[0]
    row = lambda n: pl.BlockSpec((ROW_TILE, n), lambda i: (i, 0))
    return pl.pallas_call(
        _inproj_body,
        grid=(t // ROW_TILE,),
        in_specs=[row(D_MODEL), _resident((D_MODEL, N_IN))],
        out_specs=[row(ATT_Q), row(4 * LANES), row(3 * GDN_W), row(GDN_W), row(LANES)],
        out_shape=[jax.ShapeDtypeStruct((t, ATT_Q), BF16),
                   jax.ShapeDtypeStruct((t, 4 * LANES), BF16),
                   jax.ShapeDtypeStruct((t, 3 * GDN_W), F32),
                   jax.ShapeDtypeStruct((t, GDN_W), F32),
                   jax.ShapeDtypeStruct((t, LANES), F32)],
        compiler_params=_cparams("parallel"),
        name="in_proj",
    )(x, w_all)


def _swa_body(sinks_ref, q_ref, kvc_ref, kvp_ref, o_ref):
    first = pl.program_id(1) == 0
    lane = lax.broadcasted_iota(jnp.int32, (1, LANES), 1)
    q_masks = (jnp.where(lane < ATT_HEAD_DIM, ATT_SCALE, 0.0).astype(BF16),
               jnp.where(lane >= ATT_HEAD_DIM, ATT_SCALE, 0.0).astype(BF16))
    low_half = lax.broadcasted_iota(jnp.int32, (WINDOW, LANES), 1) < ATT_HEAD_DIM
    qi = lax.broadcasted_iota(jnp.int32, (WINDOW, 2 * WINDOW), 0)
    kj = lax.broadcasted_iota(jnp.int32, (WINDOW, 2 * WINDOW), 1)
    dist = qi + WINDOW - kj
    valid = (dist >= 0) & (dist < WINDOW)
    valid_first = valid & (kj >= jnp.where(first, WINDOW, 0))
    distf = dist.astype(F32)

    for blk in range(ATT_ROWS // WINDOW):
        rows = slice(blk * WINDOW, (blk + 1) * WINDOW)
        cur = kvc_ref[rows, :]
        prev = kvp_ref[...] if blk == 0 else kvc_ref[(blk - 1) * WINDOW:blk * WINDOW, :]
        band = jnp.concatenate([prev, cur], axis=0)
        vld = valid_first if blk == 0 else valid
        for pair in range(ATT_HEADS // 2):
            kvh = (2 * pair) // ATT_GROUP
            kk = band[:, kvh * LANES:(kvh + 1) * LANES]
            vv = band[:, (2 + kvh) * LANES:(3 + kvh) * LANES]
            qp = q_ref[rows, pair * LANES:(pair + 1) * LANES]
            halves = []
            for half in range(2):
                hq = 2 * pair + half
                s = _dot_nt(qp * q_masks[half], kk)
                s = jnp.where(vld, s - ALIBI_SLOPES[hq] * distf, NEG_INF)
                sink = sinks_ref[hq]
                m = jnp.maximum(jnp.max(s, axis=-1, keepdims=True), sink)
                p = jnp.exp(s - m)
                denom = jnp.sum(p, axis=-1, keepdims=True) + jnp.exp(sink - m)
                pn = (p * (1.0 / denom)).astype(BF16)
                halves.append(_dot(pn, vv))
            o_ref[rows, pair * LANES:(pair + 1) * LANES] = jnp.where(
                low_half, halves[0], halves[1]).astype(BF16)


def _swa(aq, akv, sinks, batch, seq):
    t = aq.shape[0]
    steps = seq // ATT_ROWS
    blocks_per_step = ATT_ROWS // WINDOW
    blocks_per_seq = seq // WINDOW

    def prev_map(b, j):
        return (b * blocks_per_seq + jnp.maximum(j * blocks_per_step - 1, 0), 0)

    return pl.pallas_call(
        _swa_body,
        grid=(batch, steps),
        in_specs=[pl.BlockSpec(memory_space=pltpu.SMEM),
                  pl.BlockSpec((ATT_ROWS, ATT_Q), lambda b, j: (b * steps + j, 0)),
                  pl.BlockSpec((ATT_ROWS, 4 * LANES), lambda b, j: (b * steps + j, 0)),
                  pl.BlockSpec((WINDOW, 4 * LANES), prev_map)],
        out_specs=pl.BlockSpec((ATT_ROWS, ATT_Q), lambda b, j: (b * steps + j, 0)),
        out_shape=jax.ShapeDtypeStruct((t, ATT_Q), BF16),
        compiler_params=_cparams("parallel", "parallel"),
        name="swa_attention",
    )(sinks, aq, akv, akv)


def _split_bf16(a):
    hi = a.astype(BF16)
    lo = (a - hi.astype(F32)).astype(BF16)
    return hi, lo


def _dot3(a, b):
    ah, al = _split_bf16(a)
    bh, bl = _split_bf16(b)
    return (_dot(jnp.concatenate([ah, al], axis=1), jnp.concatenate([bh, bh], axis=0))
            + _dot(ah, bl))


def _unit_lower_inverse(a, ii, jj):
    base = 16

    def same_block(size):
        return (ii // size) == (jj // size)

    eye = (ii == jj).astype(F32)
    b1 = jnp.where(same_block(base), -a, 0.0)
    b2 = _dot3(b1, b1)
    b4 = _dot3(b2, b2)
    b8 = _dot3(b4, b4)
    t = eye + b1
    t = t + _dot3(t, b2)
    t = t + _dot3(t, b4)
    t = t + _dot3(t, b8)
    size = base
    while size < GDN_CHUNK:
        c = jnp.where(same_block(2 * size) & jnp.logical_not(same_block(size)), a, 0.0)
        t = t - _dot3(t, _dot3(c, t))
        size *= 2
    return t


def _gdn_prep_body(gp_ref, convw_ref, x_ref, xprev_ref, gab_ref,
                   qd_ref, kdt_ref, w_ref, u_ref, intra_ref, cd_ref, xs_ref):
    first = pl.program_id(1) == 0
    c_rows = GDN_CHUNK
    xs_ref[0:SUBLANES, :] = jnp.where(first, 0.0, xprev_ref[...])
    xs_ref[SUBLANES:SUBLANES + c_rows, :] = x_ref[...]

    ii = lax.broadcasted_iota(jnp.int32, (c_rows, c_rows), 0)
    jj = lax.broadcasted_iota(jnp.int32, (c_rows, c_rows), 1)
    causal = jj <= ii
    strict = jj < ii
    ltri = causal.astype(F32)

    gab = gab_ref[...]
    a_log = gp_ref[0:1, :]
    dt_bias = gp_ref[1:2, :]
    g_all = -jnp.exp(a_log) * jax.nn.softplus(gab + dt_bias)
    beta_all = jax.nn.sigmoid(gab)
    decay_cols = jnp.dot(ltri, g_all, precision=lax.Precision.HIGHEST, preferred_element_type=F32)
    g_rows = g_all.T[0:SUBLANES, :]
    decay_rows = lax.dot_general(g_rows, ltri, (((1,), (1,)), ((), ())),
                                 precision=lax.Precision.HIGHEST, preferred_element_type=F32)
    decay_last = decay_cols[c_rows - 1:c_rows, :]
    cd_ref[...] = jnp.broadcast_to(jnp.exp(decay_last), (SUBLANES, LANES))

    def conv_silu(col):
        acc = None
        for tap in range(CONV_K):
            lo = SUBLANES - (CONV_K - 1) + tap
            term = convw_ref[tap:tap + 1, col:col + LANES] * xs_ref[lo:lo + c_rows, col:col + LANES]
            acc = term if acc is None else acc + term
        return _silu(acc)

    for h in range(GDN_HEADS):
        cols = slice(h * GDN_HEAD_DIM, (h + 1) * GDN_HEAD_DIM)
        dcol = decay_cols[:, h:h + 1]
        drow = decay_rows[h:h + 1, :]
        lmask = jnp.where(causal, jnp.exp(jnp.where(causal, dcol - drow, 0.0)), 0.0)
        e_dec = jnp.exp(dcol)
        e_rem = jnp.exp(decay_last[:, h:h + 1] - dcol)
        beta = beta_all[:, GDN_HEADS + h:GDN_HEADS + h + 1]

        q = conv_silu(h * GDN_HEAD_DIM)
        k = conv_silu(GDN_W + h * GDN_HEAD_DIM)
        v = conv_silu(2 * GDN_W + h * GDN_HEAD_DIM)
        qn = q * lax.rsqrt(jnp.sum(q * q, axis=-1, keepdims=True) + RMS_EPS) * (GDN_HEAD_DIM ** -0.5)
        kn = k * lax.rsqrt(jnp.sum(k * k, axis=-1, keepdims=True) + RMS_EPS)
        kbeta = kn * beta
        kn_b = kn.astype(BF16)
        a = jnp.where(strict, _dot_nt(kbeta.astype(BF16), kn_b) * lmask, 0.0)
        intra = jnp.where(causal, _dot_nt(qn.astype(BF16), kn_b) * lmask, 0.0)
        t_inv = _unit_lower_inverse(a, ii, jj)
        rhs = jnp.concatenate([v * beta, kbeta * e_dec], axis=1).astype(BF16)
        uw = _dot(t_inv.astype(BF16), rhs)
        u_ref[:, cols] = uw[:, :GDN_HEAD_DIM]
        w_ref[:, cols] = uw[:, GDN_HEAD_DIM:].astype(BF16)
        qd_ref[:, cols] = (qn * e_dec).astype(BF16)
        kdt_ref[:, cols] = (kn * e_rem).T.astype(BF16)
        intra_ref[:, cols] = intra.astype(BF16)


def _gdn_prep(gqkv, gab, conv_w, gate_params, batch, seq):
    t = gqkv.shape[0]
    n = seq // GDN_CHUNK
    per8 = GDN_CHUNK // SUBLANES
    tile = lambda width: pl.BlockSpec((GDN_CHUNK, width), lambda b, c: (b * n + c, 0))

    def prev_map(b, c):
        return (jnp.maximum((b * n + c) * per8 - 1, 0), 0)

    head_out = lambda dt: jax.ShapeDtypeStruct((t, GDN_W), dt)
    return pl.pallas_call(
        _gdn_prep_body,
        grid=(batch, n),
        in_specs=[_resident((SUBLANES, LANES)), _resident((CONV_K, 3 * GDN_W)),
                  tile(3 * GDN_W),
                  pl.BlockSpec((SUBLANES, 3 * GDN_W), prev_map),
                  tile(LANES)],
        out_specs=[tile(GDN_W), tile(GDN_W), tile(GDN_W), tile(GDN_W), tile(GDN_W),
                   pl.BlockSpec((SUBLANES, LANES), lambda b, c: (b * n + c, 0))],
        out_shape=[head_out(BF16), head_out(BF16), head_out(BF16), head_out(F32), head_out(BF16),
                   jax.ShapeDtypeStruct((batch * n * SUBLANES, LANES), F32)],
        scratch_shapes=[pltpu.VMEM((SUBLANES + GDN_CHUNK, 3 * GDN_W), F32)],
        compiler_params=_cparams("parallel", "parallel"),
        name="gdn_prep",
    )(gate_params, conv_w, gqkv, gqkv, gab)


def _gdn_scan_body(normg_ref, cd_ref, qd_ref, kdt_ref, w_ref, u_ref, intra_ref, z_ref,
                   o_ref, state_ref):
    @pl.when(pl.program_id(1) == 0)
    def _():
        state_ref[...] = jnp.zeros_like(state_ref)

    for h in range(GDN_HEADS):
        cols = slice(h * GDN_HEAD_DIM, (h + 1) * GDN_HEAD_DIM)
        state = state_ref[h]
        wq = jnp.concatenate([w_ref[:, cols], qd_ref[:, cols]], axis=0)
        r = _dot(wq, state.astype(BF16))
        v_new = u_ref[:, cols] - r[:GDN_CHUNK]
        v_new_b = v_new.astype(BF16)
        o = r[GDN_CHUNK:] + _dot(intra_ref[:, cols], v_new_b)
        state_ref[h] = state * cd_ref[0:1, h:h + 1] + _dot(kdt_ref[:, cols], v_new_b)
        o = (o * lax.rsqrt(jnp.mean(o * o, axis=-1, keepdims=True) + RMS_EPS)
             * normg_ref[...] * _silu(z_ref[:, cols]))
        o_ref[:, cols] = o.astype(BF16)


def _gdn_scan(norm_g, cd, qd, kdt, w, u, intra, gz, batch, seq):
    t = qd.shape[0]
    n = seq // GDN_CHUNK
    tile = pl.BlockSpec((GDN_CHUNK, GDN_W), lambda b, c: (b * n + c, 0))
    return pl.pallas_call(
        _gdn_scan_body,
        grid=(batch, n),
        in_specs=[_resident((1, GDN_HEAD_DIM)),
                  pl.BlockSpec((SUBLANES, LANES), lambda b, c: (b * n + c, 0)),
                  tile, tile, tile, tile, tile, tile],
        out_specs=tile,
        out_shape=jax.ShapeDtypeStruct((t, GDN_W), BF16),
        scratch_shapes=[pltpu.VMEM((GDN_HEADS, GDN_HEAD_DIM, GDN_HEAD_DIM), F32)],
        compiler_params=_cparams("parallel", "arbitrary"),
        name="gdn_scan",
    )(norm_g, cd, qd, kdt, w, u, intra, gz)


def _outproj_ln_body(att_ref, gdn_ref, w_ref, x_ref, g_ref, b_ref, o_ref):
    y = _dot(att_ref[...], w_ref[0:ATT_Q, :]) + _dot(gdn_ref[...], w_ref[ATT_Q:ATT_Q + GDN_W, :])
    o_ref[...] = _layer_norm(ALPHA * x_ref[...] + y, g_ref[...], b_ref[...])


def _outproj_ln(att, gdn, w_out, x, g, b):
    t = x.shape[0]
    row = lambda n: pl.BlockSpec((ROW_TILE, n), lambda i: (i, 0))
    return pl.pallas_call(
        _outproj_ln_body,
        grid=(t // ROW_TILE,),
        in_specs=[row(ATT_Q), row(GDN_W), _resident((D_MODEL, D_MODEL)), row(D_MODEL),
                  _resident((1, D_MODEL)), _resident((1, D_MODEL))],
        out_specs=row(D_MODEL),
        out_shape=jax.ShapeDtypeStruct((t, D_MODEL), F32),
        compiler_params=_cparams("parallel"),
        name="out_proj_ln",
    )(att, gdn, w_out, x, g, b)


def _memkv_body(m_ref, w_ref, k_ref, v_ref):
    mb = m_ref[...].astype(BF16)
    k_ref[...] = _dot(mb, w_ref[:, 0:D_MODEL]).astype(BF16)
    v_ref[...] = _dot(mb, w_ref[:, D_MODEL:2 * D_MODEL]).astype(BF16)


def _memkv(mem, w_kv):
    rows = mem.shape[0]
    tile = min(ROW_TILE, rows)
    row = pl.BlockSpec((tile, D_MODEL), lambda i: (i, 0))
    return pl.pallas_call(
        _memkv_body,
        grid=(rows // tile,),
        in_specs=[row, _resident((D_MODEL, 2 * D_MODEL))],
        out_specs=[row, row],
        out_shape=[jax.ShapeDtypeStruct((rows, D_MODEL), BF16)] * 2,
        compiler_params=_cparams("parallel"),
        name="mem_kv_proj",
    )(mem, w_kv)


def _xattn_ln_body(x_ref, wq_ref, k_ref, v_ref, wo_ref, g_ref, b_ref, o_ref, heads_ref):
    x = x_ref[...]
    xb = x.astype(BF16)
    for h in range(X_HEADS):
        cols = slice(h * X_HEAD_DIM, (h + 1) * X_HEAD_DIM)
        qh = (_dot(xb, wq_ref[:, cols]) * X_SCALE).astype(BF16)
        s = _dot_nt(qh, k_ref[:, cols])
        m = jnp.max(s, axis=-1, keepdims=True)
        p = jnp.exp(s - m)
        pn = (p * (1.0 / jnp.sum(p, axis=-1, keepdims=True))).astype(BF16)
        heads_ref[:, cols] = _dot(pn, v_ref[:, cols]).astype(BF16)
    y = _dot(heads_ref[...], wo_ref[...])
    o_ref[...] = _layer_norm(ALPHA * x + y, g_ref[...], b_ref[...])


def _xattn_ln(x, wq, k_mem, v_mem, wo, g, b, batch, seq):
    t = x.shape[0]
    steps = seq // ROW_TILE
    mem_len = k_mem.shape[0] // batch
    row = pl.BlockSpec((ROW_TILE, D_MODEL), lambda bb, i: (bb * steps + i, 0))
    mem = pl.BlockSpec((mem_len, D_MODEL), lambda bb, i: (bb, 0))
    return pl.pallas_call(
        _xattn_ln_body,
        grid=(batch, steps),
        in_specs=[row, _resident((D_MODEL, D_MODEL)), mem, mem, _resident((D_MODEL, D_MODEL)),
                  _resident((1, D_MODEL)), _resident((1, D_MODEL))],
        out_specs=row,
        out_shape=jax.ShapeDtypeStruct((t, D_MODEL), F32),
        scratch_shapes=[pltpu.VMEM((ROW_TILE, D_MODEL), BF16)],
        compiler_params=_cparams("parallel", "parallel"),
        name="mem_xattn_ln",
    )(x, wq, k_mem, v_mem, wo, g, b)


def _mlp_ln_body(x_ref, w1_ref, w2_ref, g_ref, b_ref, o_ref):
    x = x_ref[...]
    xb = x.astype(BF16)
    y = None
    for c in range(D_FF // FF_CHUNK):
        cols = slice(c * FF_CHUNK, (c + 1) * FF_CHUNK)
        hid = jnp.square(jnp.maximum(_dot(xb, w1_ref[:, cols]), 0.0)).astype(BF16)
        part = _dot(hid, w2_ref[cols, :])
        y = part if y is None else y + part
    o_ref[...] = _layer_norm(ALPHA * x + y, g_ref[...], b_ref[...])


def _mlp_ln(x, w1, w2, g, b):
    t = x.shape[0]
    row = pl.BlockSpec((ROW_TILE, D_MODEL), lambda i: (i, 0))
    return pl.pallas_call(
        _mlp_ln_body,
        grid=(t // ROW_TILE,),
        in_specs=[row, _resident((D_MODEL, D_FF)), _resident((D_FF, D_MODEL)),
                  _resident((1, D_MODEL)), _resident((1, D_MODEL))],
        out_specs=row,
        out_shape=jax.ShapeDtypeStruct((t, D_MODEL), F32),
        compiler_params=_cparams("parallel"),
        name="mlp_ln",
    )(x, w1, w2, g, b)


def _pack_w_in(w_in):
    o = 0
    aq = w_in[:, o:o + ATT_Q]; o += ATT_Q
    ak = w_in[:, o:o + ATT_KV]; o += ATT_KV
    av = w_in[:, o:o + ATT_KV]; o += ATT_KV
    gqkv = w_in[:, o:o + 3 * GDN_W]; o += 3 * GDN_W
    ga = w_in[:, o:o + GDN_HEADS]; o += GDN_HEADS
    gb = w_in[:, o:o + GDN_HEADS]; o += GDN_HEADS
    gz = w_in[:, o:o + GDN_W]
    dup = lambda m: jnp.concatenate(
        [m[:, h * ATT_HEAD_DIM:(h + 1) * ATT_HEAD_DIM] for h in range(ATT_KV_HEADS) for _ in range(2)], axis=1)
    pad = jnp.zeros((w_in.shape[0], LANES - 2 * GDN_HEADS), w_in.dtype)
    return jnp.concatenate([aq, dup(ak), dup(av), gqkv, gz, ga, gb, pad], axis=1).astype(BF16)


def _pad_lanes(v):
    return jnp.pad(v, (0, LANES - v.shape[0]))


def kernel(x, mem, w_in, conv_w, attn_sinks, a_log, dt_bias, gdn_norm_g, w_mix_out,
           wq_mem, wk_mem, wv_mem, wo_mem, w_ff1, w_ff2, ln_g, ln_b):
    batch, seq, _ = x.shape
    xt = x.reshape(batch * seq, D_MODEL)
    memt = mem.reshape(batch * mem.shape[1], D_MODEL)
    for l in range(DEPTH):
        w_all = _pack_w_in(w_in[l])
        gate_params = jnp.zeros((SUBLANES, LANES), F32)
        gate_params = gate_params.at[0].set(_pad_lanes(a_log[l])).at[1].set(_pad_lanes(dt_bias[l]))
        ln = lambda i: (ln_g[l, i].reshape(1, D_MODEL), ln_b[l, i].reshape(1, D_MODEL))

        aq, akv, gqkv, gz, gab = _inproj(xt, w_all)
        att = _swa(aq, akv, attn_sinks[l], batch, seq)
        qd, kdt, w, u, intra, cd = _gdn_prep(gqkv, gab, conv_w[l], gate_params, batch, seq)
        gdn = _gdn_scan(gdn_norm_g[l].reshape(1, GDN_HEAD_DIM), cd, qd, kdt, w, u, intra, gz, batch, seq)
        xt = _outproj_ln(att, gdn, w_mix_out[l].astype(BF16), xt, *ln(0))

        w_kv = jnp.concatenate([wk_mem[l], wv_mem[l]], axis=1).astype(BF16)
        k_mem, v_mem = _memkv(memt, w_kv)
        xt = _xattn_ln(xt, wq_mem[l].astype(BF16), k_mem, v_mem, wo_mem[l].astype(BF16), *ln(1), batch, seq)

        xt = _mlp_ln(xt, w_ff1[l].astype(BF16), w_ff2[l].astype(BF16), *ln(2))
    return xt.reshape(batch, seq, D_MODEL)
```

```python
import functools

import jax
import jax.numpy as jnp
from jax import lax
from jax.experimental import pallas as pl
from jax.experimental.pallas import tpu as pltpu

F32 = jnp.float32
BF16 = jnp.bfloat16

D_MODEL = 1024
DEPTH = 2
ATT_HEADS = 8
ATT_KV_HEADS = 2
ATT_HEAD_DIM = 64
ATT_GROUP = ATT_HEADS // ATT_KV_HEADS
WINDOW = 128
ATT_Q = ATT_HEADS * ATT_HEAD_DIM
ATT_KV = ATT_KV_HEADS * ATT_HEAD_DIM
GDN_HEADS = 4
GDN_HEAD_DIM = 128
GDN_W = GDN_HEADS * GDN_HEAD_DIM
CONV_K = 4
X_HEADS = 4
X_HEAD_DIM = D_MODEL // X_HEADS
D_FF = 4 * D_MODEL
ALPHA = (2 * DEPTH) ** 0.25
LN_EPS = 1e-5
RMS_EPS = 1e-6
NEG_INF = -1e30
ATT_SCALE = ATT_HEAD_DIM ** -0.5
X_SCALE = X_HEAD_DIM ** -0.5
ALIBI_SLOPES = tuple(2.0 ** (-8.0 * (h + 1) / ATT_HEADS) for h in range(ATT_HEADS))

LANES = 128
SUBLANES = 8
ROW_TILE = 512
ATT_ROWS = 512
GDN_CHUNK = 128
FF_CHUNK = 512
VMEM_LIMIT = 56 * 1024 * 1024

_C_AQ = (0, ATT_Q)
_C_AKV = (_C_AQ[1], _C_AQ[1] + 4 * LANES)
_C_GQKV = (_C_AKV[1], _C_AKV[1] + 3 * GDN_W)
_C_GZ = (_C_GQKV[1], _C_GQKV[1] + GDN_W)
_C_GAB = (_C_GZ[1], _C_GZ[1] + LANES)
N_IN = _C_GAB[1]


def _cparams(*sem):
    return pltpu.CompilerParams(dimension_semantics=sem, vmem_limit_bytes=VMEM_LIMIT)


def _resident(shape):
    nd = len(shape)
    return pl.BlockSpec(shape, lambda *_: (0,) * nd, pipeline_mode=pl.Buffered(1))


def _dot(a, b):
    return jnp.dot(a, b, preferred_element_type=F32)


def _dot_nt(a, b):
    return lax.dot_general(a, b, (((1,), (1,)), ((), ())), preferred_element_type=F32)


def _layer_norm(xf, g, b):
    mu = jnp.mean(xf, axis=-1, keepdims=True)
    d = xf - mu
    var = jnp.mean(d * d, axis=-1, keepdims=True)
    return d * lax.rsqrt(var + LN_EPS) * g + b


def _silu(x):
    return x * jax.nn.sigmoid(x)


def _inproj_body(x_ref, w_ref, aq_ref, akv_ref, gqkv_ref, gz_ref, gab_ref):
    xb = x_ref[...].astype(BF16)

    def mm(lo, hi):
        return _dot(xb, w_ref[:, lo:hi])

    aq_ref[...] = mm(*_C_AQ).astype(BF16)
    akv_ref[...] = mm(*_C_AKV).astype(BF16)
    for c in range(3):
        lo = _C_GQKV[0] + c * GDN_W
        gqkv_ref[:, c * GDN_W:(c + 1) * GDN_W] = mm(lo, lo + GDN_W)
    gz_ref[...] = mm(*_C_GZ)
    gab_ref[...] = mm(*_C_GAB)


def _inproj(x, w_all):
    t = x.shape[0]
    row = lambda n: pl.BlockSpec((ROW_TILE, n), lambda i: (i, 0))
    return pl.pallas_call(
        _inproj_body,
        grid=(t // ROW_TILE,),
        in_specs=[row(D_MODEL), _resident((D_MODEL, N_IN))],
        out_specs=[row(ATT_Q), row(4 * LANES), row(3 * GDN_W), row(GDN_W), row(LANES)],
        out_shape=[jax.ShapeDtypeStruct((t, ATT_Q), BF16),
                   jax.ShapeDtypeStruct((t, 4 * LANES), BF16),
                   jax.ShapeDtypeStruct((t, 3 * GDN_W), F32),
                   jax.ShapeDtypeStruct((t, GDN_W), F32),
                   jax.ShapeDtypeStruct((t, LANES), F32)],
        compiler_params=_cparams("parallel"),
        name="in_proj",
    )(x, w_all)


def _swa_body(sinks_ref, q_ref, kvc_ref, kvp_ref, o_ref):
    first = pl.program_id(1) == 0
    lane = lax.broadcasted_iota(jnp.int32, (1, LANES), 1)
    q_masks = (jnp.where(lane < ATT_HEAD_DIM, ATT_SCALE, 0.0).astype(BF16),
               jnp.where(lane >= ATT_HEAD_DIM, ATT_SCALE, 0.0).astype(BF16))
    low_half = lax.broadcasted_iota(jnp.int32, (WINDOW, LANES), 1) < ATT_HEAD_DIM
    qi = lax.broadcasted_iota(jnp.int32, (WINDOW, 2 * WINDOW), 0)
    kj = lax.broadcasted_iota(jnp.int32, (WINDOW, 2 * WINDOW), 1)
    dist = qi + WINDOW - kj
    valid = (dist >= 0) & (dist < WINDOW)
    valid_first = valid & (kj >= jnp.where(first, WINDOW, 0))
    distf = dist.astype(F32)

    for blk in range(ATT_ROWS // WINDOW):
        rows = slice(blk * WINDOW, (blk + 1) * WINDOW)
        cur = kvc_ref[rows, :]
        prev = kvp_ref[...] if blk == 0 else kvc_ref[(blk - 1) * WINDOW:blk * WINDOW, :]
        band = jnp.concatenate([prev, cur], axis=0)
        vld = valid_first if blk == 0 else valid
        for pair in range(ATT_HEADS // 2):
            kvh = (2 * pair) // ATT_GROUP
            kk = band[:, kvh * LANES:(kvh + 1) * LANES]
            vv = band[:, (2 + kvh) * LANES:(3 + kvh) * LANES]
            qp = q_ref[rows, pair * LANES:(pair + 1) * LANES]
            halves = []
            for half in range(2):
                hq = 2 * pair + half
                s = _dot_nt(qp * q_masks[half], kk)
                s = jnp.where(vld, s - ALIBI_SLOPES[hq] * distf, NEG_INF)
                sink = sinks_ref[hq]
                m = jnp.maximum(jnp.max(s, axis=-1, keepdims=True), sink)
                p = jnp.exp(s - m)
                denom = jnp.sum(p, axis=-1, keepdims=True) + jnp.exp(sink - m)
                pn = (p * (1.0 / denom)).astype(BF16)
                halves.append(_dot(pn, vv))
            o_ref[rows, pair * LANES:(pair + 1) * LANES] = jnp.where(
                low_half, halves[0], halves[1]).astype(BF16)


def _swa(aq, akv, sinks, batch, seq):
    t = aq.shape[0]
    steps = seq // ATT_ROWS
    blocks_per_step = ATT_ROWS // WINDOW
    blocks_per_seq = seq // WINDOW

    def prev_map(b, j):
        return (b * blocks_per_seq + jnp.maximum(j * blocks_per_step - 1, 0), 0)

    return pl.pallas_call(
        _swa_body,
        grid=(batch, steps),
        in_specs=[pl.BlockSpec(memory_space=pltpu.SMEM),
                  pl.BlockSpec((ATT_ROWS, ATT_Q), lambda b, j: (b * steps + j, 0)),
                  pl.BlockSpec((ATT_ROWS, 4 * LANES), lambda b, j: (b * steps + j, 0)),
                  pl.BlockSpec((WINDOW, 4 * LANES), prev_map)],
        out_specs=pl.BlockSpec((ATT_ROWS, ATT_Q), lambda b, j: (b * steps + j, 0)),
        out_shape=jax.ShapeDtypeStruct((t, ATT_Q), BF16),
        compiler_params=_cparams("parallel", "parallel"),
        name="swa_attention",
    )(sinks, aq, akv, akv)


def _unit_lower_inverse(mats, ii, jj):
    base = 16

    def same_block(size):
        return (ii // size) == (jj // size)

    def bf(ms):
        return [m.astype(BF16) for m in ms]

    eye = (ii == jj).astype(F32)
    in_base = same_block(base)
    b1 = [jnp.where(in_base, -a, 0.0) for a in mats]
    b1_b = bf(b1)
    b2_b = bf([_dot(b, b) for b in b1_b])
    t = [eye + b for b in b1]
    t = [x + _dot(xb, p) for x, xb, p in zip(t, bf(t), b2_b)]
    b4_b = bf([_dot(b, b) for b in b2_b])
    t = [x + _dot(xb, p) for x, xb, p in zip(t, bf(t), b4_b)]
    b8_b = bf([_dot(b, b) for b in b4_b])
    t = [x + _dot(xb, p) for x, xb, p in zip(t, bf(t), b8_b)]
    size = base
    while size < GDN_CHUNK:
        off_diag = same_block(2 * size) & jnp.logical_not(same_block(size))
        t_b = bf(t)
        ct_b = bf([_dot(jnp.where(off_diag, a, 0.0).astype(BF16), xb) for a, xb in zip(mats, t_b)])
        t = [x - _dot(xb, y) for x, xb, y in zip(t, t_b, ct_b)]
        size *= 2
    return t


def _gdn_prep_body(gp_ref, convw_ref, x_ref, xprev_ref, gab_ref,
                   qd_ref, kdt_ref, w_ref, u_ref, intra_ref, cd_ref, xs_ref):
    first = pl.program_id(1) == 0
    c_rows = GDN_CHUNK
    xs_ref[0:SUBLANES, :] = jnp.where(first, 0.0, xprev_ref[...])
    xs_ref[SUBLANES:SUBLANES + c_rows, :] = x_ref[...]

    ii = lax.broadcasted_iota(jnp.int32, (c_rows, c_rows), 0)
    jj = lax.broadcasted_iota(jnp.int32, (c_rows, c_rows), 1)
    causal = jj <= ii
    strict = jj < ii
    ltri = causal.astype(F32)

    gab = gab_ref[...]
    a_log = gp_ref[0:1, :]
    dt_bias = gp_ref[1:2, :]
    g_all = -jnp.exp(a_log) * jax.nn.softplus(gab + dt_bias)
    beta_all = jax.nn.sigmoid(gab)
    decay_cols = jnp.dot(ltri, g_all, precision=lax.Precision.HIGHEST, preferred_element_type=F32)
    g_rows = g_all.T[0:SUBLANES, :]
    decay_rows = lax.dot_general(g_rows, ltri, (((1,), (1,)), ((), ())),
                                 precision=lax.Precision.HIGHEST, preferred_element_type=F32)
    decay_last = decay_cols[c_rows - 1:c_rows, :]
    cd_ref[...] = jnp.broadcast_to(jnp.exp(decay_last), (SUBLANES, LANES))

    def conv_silu(col):
        acc = None
        for tap in range(CONV_K):
            lo = SUBLANES - (CONV_K - 1) + tap
            term = convw_ref[tap:tap + 1, col:col + LANES] * xs_ref[lo:lo + c_rows, col:col + LANES]
            acc = term if acc is None else acc + term
        return _silu(acc)

    heads = range(GDN_HEADS)
    a_mats, rhs = [], []
    for h in heads:
        cols = slice(h * GDN_HEAD_DIM, (h + 1) * GDN_HEAD_DIM)
        dcol = decay_cols[:, h:h + 1]
        drow = decay_rows[h:h + 1, :]
        lmask = jnp.where(causal, jnp.exp(jnp.where(causal, dcol - drow, 0.0)), 0.0)
        e_dec = jnp.exp(dcol)
        e_rem = jnp.exp(decay_last[:, h:h + 1] - dcol)
        beta = beta_all[:, GDN_HEADS + h:GDN_HEADS + h + 1]

        q = conv_silu(h * GDN_HEAD_DIM)
        k = conv_silu(GDN_W + h * GDN_HEAD_DIM)
        v = conv_silu(2 * GDN_W + h * GDN_HEAD_DIM)
        qn = q * lax.rsqrt(jnp.sum(q * q, axis=-1, keepdims=True) + RMS_EPS) * (GDN_HEAD_DIM ** -0.5)
        kn = k * lax.rsqrt(jnp.sum(k * k, axis=-1, keepdims=True) + RMS_EPS)
        kbeta = kn * beta
        kn_b = kn.astype(BF16)
        a_mats.append(jnp.where(strict, _dot_nt(kbeta.astype(BF16), kn_b) * lmask, 0.0))
        intra = jnp.where(causal, _dot_nt(qn.astype(BF16), kn_b) * lmask, 0.0)
        rhs.append(jnp.concatenate([v * beta, kbeta * e_dec], axis=1).astype(BF16))
        qd_ref[:, cols] = (qn * e_dec).astype(BF16)
        kdt_ref[:, cols] = (kn * e_rem).T.astype(BF16)
        intra_ref[:, cols] = intra.astype(BF16)

    t_inv = _unit_lower_inverse(a_mats, ii, jj)
    for h in heads:
        cols = slice(h * GDN_HEAD_DIM, (h + 1) * GDN_HEAD_DIM)
        uw = _dot(t_inv[h].astype(BF16), rhs[h])
        u_ref[:, cols] = uw[:, :GDN_HEAD_DIM]
        w_ref[:, cols] = uw[:, GDN_HEAD_DIM:].astype(BF16)


def _gdn_prep(gqkv, gab, conv_w, gate_params, batch, seq):
    t = gqkv.shape[0]
    n = seq // GDN_CHUNK
    per8 = GDN_CHUNK // SUBLANES
    tile = lambda width: pl.BlockSpec((GDN_CHUNK, width), lambda b, c: (b * n + c, 0))

    def prev_map(b, c):
        return (jnp.maximum((b * n + c) * per8 - 1, 0), 0)

    head_out = lambda dt: jax.ShapeDtypeStruct((t, GDN_W), dt)
    return pl.pallas_call(
        _gdn_prep_body,
        grid=(batch, n),
        in_specs=[_resident((SUBLANES, LANES)), _resident((CONV_K, 3 * GDN_W)),
                  tile(3 * GDN_W),
                  pl.BlockSpec((SUBLANES, 3 * GDN_W), prev_map),
                  tile(LANES)],
        out_specs=[tile(GDN_W), tile(GDN_W), tile(GDN_W), tile(GDN_W), tile(GDN_W),
                   pl.BlockSpec((SUBLANES, LANES), lambda b, c: (b * n + c, 0))],
        out_shape=[head_out(BF16), head_out(BF16), head_out(BF16), head_out(F32), head_out(BF16),
                   jax.ShapeDtypeStruct((batch * n * SUBLANES, LANES), F32)],
        scratch_shapes=[pltpu.VMEM((SUBLANES + GDN_CHUNK, 3 * GDN_W), F32)],
        compiler_params=_cparams("parallel", "parallel"),
        name="gdn_prep",
    )(gate_params, conv_w, gqkv, gqkv, gab)


def _gdn_scan_body(normg_ref, cd_ref, qd_ref, kdt_ref, w_ref, u_ref, intra_ref, z_ref,
                   o_ref, state_ref):
    @pl.when(pl.program_id(1) == 0)
    def _():
        state_ref[...] = jnp.zeros_like(state_ref)

    for h in range(GDN_HEADS):
        cols = slice(h * GDN_HEAD_DIM, (h + 1) * GDN_HEAD_DIM)
        state = state_ref[h]
        wq = jnp.concatenate([w_ref[:, cols], qd_ref[:, cols]], axis=0)
        r = _dot(wq, state.astype(BF16))
        v_new = u_ref[:, cols] - r[:GDN_CHUNK]
        v_new_b = v_new.astype(BF16)
        o = r[GDN_CHUNK:] + _dot(intra_ref[:, cols], v_new_b)
        state_ref[h] = state * cd_ref[0:1, h:h + 1] + _dot(kdt_ref[:, cols], v_new_b)
        o = (o * lax.rsqrt(jnp.mean(o * o, axis=-1, keepdims=True) + RMS_EPS)
             * normg_ref[...] * _silu(z_ref[:, cols]))
        o_ref[:, cols] = o.astype(BF16)


def _gdn_scan(norm_g, cd, qd, kdt, w, u, intra, gz, batch, seq):
    t = qd.shape[0]
    n = seq // GDN_CHUNK
    tile = pl.BlockSpec((GDN_CHUNK, GDN_W), lambda b, c: (b * n + c, 0))
    return pl.pallas_call(
        _gdn_scan_body,
        grid=(batch, n),
        in_specs=[_resident((1, GDN_HEAD_DIM)),
                  pl.BlockSpec((SUBLANES, LANES), lambda b, c: (b * n + c, 0)),
                  tile, tile, tile, tile, tile, tile],
        out_specs=tile,
        out_shape=jax.ShapeDtypeStruct((t, GDN_W), BF16),
        scratch_shapes=[pltpu.VMEM((GDN_HEADS, GDN_HEAD_DIM, GDN_HEAD_DIM), F32)],
        compiler_params=_cparams("parallel", "arbitrary"),
        name="gdn_scan",
    )(norm_g, cd, qd, kdt, w, u, intra, gz)


def _outproj_ln_body(att_ref, gdn_ref, w_ref, x_ref, g_ref, b_ref, o_ref):
    y = _dot(att_ref[...], w_ref[0:ATT_Q, :]) + _dot(gdn_ref[...], w_ref[ATT_Q:ATT_Q + GDN_W, :])
    o_ref[...] = _layer_norm(ALPHA * x_ref[...] + y, g_ref[...], b_ref[...])


def _outproj_ln(att, gdn, w_out, x, g, b):
    t = x.shape[0]
    row = lambda n: pl.BlockSpec((ROW_TILE, n), lambda i: (i, 0))
    return pl.pallas_call(
        _outproj_ln_body,
        grid=(t // ROW_TILE,),
        in_specs=[row(ATT_Q), row(GDN_W), _resident((D_MODEL, D_MODEL)), row(D_MODEL),
                  _resident((1, D_MODEL)), _resident((1, D_MODEL))],
        out_specs=row(D_MODEL),
        out_shape=jax.ShapeDtypeStruct((t, D_MODEL), F32),
        compiler_params=_cparams("parallel"),
        name="out_proj_ln",
    )(att, gdn, w_out, x, g, b)


def _memkv_body(m_ref, w_ref, k_ref, v_ref):
    mb = m_ref[...].astype(BF16)
    k_ref[...] = _dot(mb, w_ref[:, 0:D_MODEL]).astype(BF16)
    v_ref[...] = _dot(mb, w_ref[:, D_MODEL:2 * D_MODEL]).astype(BF16)


def _memkv(mem, w_kv):
    rows = mem.shape[0]
    tile = min(ROW_TILE, rows)
    row = pl.BlockSpec((tile, D_MODEL), lambda i: (i, 0))
    return pl.pallas_call(
        _memkv_body,
        grid=(rows // tile,),
        in_specs=[row, _resident((D_MODEL, 2 * D_MODEL))],
        out_specs=[row, row],
        out_shape=[jax.ShapeDtypeStruct((rows, D_MODEL), BF16)] * 2,
        compiler_params=_cparams("parallel"),
        name="mem_kv_proj",
    )(mem, w_kv)


def _xattn_ln_body(x_ref, wq_ref, k_ref, v_ref, wo_ref, g_ref, b_ref, o_ref, heads_ref):
    x = x_ref[...]
    xb = x.astype(BF16)
    for h in range(X_HEADS):
        cols = slice(h * X_HEAD_DIM, (h + 1) * X_HEAD_DIM)
        qh = (_dot(xb, wq_ref[:, cols]) * X_SCALE).astype(BF16)
        s = _dot_nt(qh, k_ref[:, cols])
        m = jnp.max(s, axis=-1, keepdims=True)
        p = jnp.exp(s - m)
        pn = (p * (1.0 / jnp.sum(p, axis=-1, keepdims=True))).astype(BF16)
        heads_ref[:, cols] = _dot(pn, v_ref[:, cols]).astype(BF16)
    y = _dot(heads_ref[...], wo_ref[...])
    o_ref[...] = _layer_norm(ALPHA * x + y, g_ref[...], b_ref[...])


def _xattn_ln(x, wq, k_mem, v_mem, wo, g, b, batch, seq):
    t = x.shape[0]
    steps = seq // ROW_TILE
    mem_len = k_mem.shape[0] // batch
    row = pl.BlockSpec((ROW_TILE, D_MODEL), lambda bb, i: (bb * steps + i, 0))
    mem = pl.BlockSpec((mem_len, D_MODEL), lambda bb, i: (bb, 0))
    return pl.pallas_call(
        _xattn_ln_body,
        grid=(batch, steps),
        in_specs=[row, _resident((D_MODEL, D_MODEL)), mem, mem, _resident((D_MODEL, D_MODEL)),
                  _resident((1, D_MODEL)), _resident((1, D_MODEL))],
        out_specs=row,
        out_shape=jax.ShapeDtypeStruct((t, D_MODEL), F32),
        scratch_shapes=[pltpu.VMEM((ROW_TILE, D_MODEL), BF16)],
        compiler_params=_cparams("parallel", "parallel"),
        name="mem_xattn_ln",
    )(x, wq, k_mem, v_mem, wo, g, b)


def _mlp_ln_body(x_ref, w1_ref, w2_ref, g_ref, b_ref, o_ref):
    x = x_ref[...]
    xb = x.astype(BF16)
    y = None
    for c in range(D_FF // FF_CHUNK):
        cols = slice(c * FF_CHUNK, (c + 1) * FF_CHUNK)
        hid = jnp.square(jnp.maximum(_dot(xb, w1_ref[:, cols]), 0.0)).astype(BF16)
        part = _dot(hid, w2_ref[cols, :])
        y = part if y is None else y + part
    o_ref[...] = _layer_norm(ALPHA * x + y, g_ref[...], b_ref[...])


def _mlp_ln(x, w1, w2, g, b):
    t = x.shape[0]
    row = pl.BlockSpec((ROW_TILE, D_MODEL), lambda i: (i, 0))
    return pl.pallas_call(
        _mlp_ln_body,
        grid=(t // ROW_TILE,),
        in_specs=[row, _resident((D_MODEL, D_FF)), _resident((D_FF, D_MODEL)),
                  _resident((1, D_MODEL)), _resident((1, D_MODEL))],
        out_specs=row,
        out_shape=jax.ShapeDtypeStruct((t, D_MODEL), F32),
        compiler_params=_cparams("parallel"),
        name="mlp_ln",
    )(x, w1, w2, g, b)


def _pack_w_in(w_in):
    o = 0
    aq = w_in[:, o:o + ATT_Q]; o += ATT_Q
    ak = w_in[:, o:o + ATT_KV]; o += ATT_KV
    av = w_in[:, o:o + ATT_KV]; o += ATT_KV
    gqkv = w_in[:, o:o + 3 * GDN_W]; o += 3 * GDN_W
    ga = w_in[:, o:o + GDN_HEADS]; o += GDN_HEADS
    gb = w_in[:, o:o + GDN_HEADS]; o += GDN_HEADS
    gz = w_in[:, o:o + GDN_W]
    dup = lambda m: jnp.concatenate(
        [m[:, h * ATT_HEAD_DIM:(h + 1) * ATT_HEAD_DIM] for h in range(ATT_KV_HEADS) for _ in range(2)], axis=1)
    pad = jnp.zeros((w_in.shape[0], LANES - 2 * GDN_HEADS), w_in.dtype)
    return jnp.concatenate([aq, dup(ak), dup(av), gqkv, gz, ga, gb, pad], axis=1).astype(BF16)


def _pad_lanes(v):
    return jnp.pad(v, (0, LANES - v.shape[0]))


def kernel(x, mem, w_in, conv_w, attn_sinks, a_log, dt_bias, gdn_norm_g, w_mix_out,
           wq_mem, wk_mem, wv_mem, wo_mem, w_ff1, w_ff2, ln_g, ln_b):
    batch, seq, _ = x.shape
    xt = x.reshape(batch * seq, D_MODEL)
    memt = mem.reshape(batch * mem.shape[1], D_MODEL)
    for l in range(DEPTH):
        w_all = _pack_w_in(w_in[l])
        gate_params = jnp.zeros((SUBLANES, LANES), F32)
        gate_params = gate_params.at[0].set(_pad_lanes(a_log[l])).at[1].set(_pad_lanes(dt_bias[l]))
        ln = lambda i: (ln_g[l, i].reshape(1, D_MODEL), ln_b[l, i].reshape(1, D_MODEL))

        aq, akv, gqkv, gz, gab = _inproj(xt, w_all)
        att = _swa(aq, akv, attn_sinks[l], batch, seq)
        qd, kdt, w, u, intra, cd = _gdn_prep(gqkv, gab, conv_w[l], gate_params, batch, seq)
        gdn = _gdn_scan(gdn_norm_g[l].reshape(1, GDN_HEAD_DIM), cd, qd, kdt, w, u, intra, gz, batch, seq)
        xt = _outproj_ln(att, gdn, w_mix_out[l].astype(BF16), xt, *ln(0))

        w_kv = jnp.concatenate([wk_mem[l], wv_mem[l]], axis=1).astype(BF16)
        k_mem, v_mem = _memkv(memt, w_kv)
        xt = _xattn_ln(xt, wq_mem[l].astype(BF16), k_mem, v_mem, wo_mem[l].astype(BF16), *ln(1), batch, seq)

        xt = _mlp_ln(xt, w_ff1[l].astype(BF16), w_ff2[l].astype(BF16), *ln(2))
    return xt.reshape(batch, seq, D_MODEL)
```

```python
import functools

import jax
import jax.numpy as jnp
from jax import lax
from jax.experimental import pallas as pl
from jax.experimental.pallas import tpu as pltpu

F32 = jnp.float32
BF16 = jnp.bfloat16

D_MODEL = 1024
DEPTH = 2
ATT_HEADS = 8
ATT_KV_HEADS = 2
ATT_HEAD_DIM = 64
ATT_GROUP = ATT_HEADS // ATT_KV_HEADS
WINDOW = 128
ATT_Q = ATT_HEADS * ATT_HEAD_DIM
ATT_KV = ATT_KV_HEADS * ATT_HEAD_DIM
GDN_HEADS = 4
GDN_HEAD_DIM = 128
GDN_W = GDN_HEADS * GDN_HEAD_DIM
CONV_K = 4
X_HEADS = 4
X_HEAD_DIM = D_MODEL // X_HEADS
D_FF = 4 * D_MODEL
ALPHA = (2 * DEPTH) ** 0.25
LN_EPS = 1e-5
RMS_EPS = 1e-6
NEG_INF = -1e30
ATT_SCALE = ATT_HEAD_DIM ** -0.5
X_SCALE = X_HEAD_DIM ** -0.5
ALIBI_SLOPES = tuple(2.0 ** (-8.0 * (h + 1) / ATT_HEADS) for h in range(ATT_HEADS))

LANES = 128
SUBLANES = 8
ROW_TILE = 512
SUB_ROWS = 256
ATT_ROWS = 512
GDN_CHUNK = 128
FF_CHUNK = 512
VMEM_LIMIT = 56 * 1024 * 1024

_C_AQ = (0, ATT_Q)
_C_AKV = (_C_AQ[1], _C_AQ[1] + 4 * LANES)
_C_GQKV = (_C_AKV[1], _C_AKV[1] + 3 * GDN_W)
_C_GZ = (_C_GQKV[1], _C_GQKV[1] + GDN_W)
_C_GAB = (_C_GZ[1], _C_GZ[1] + LANES)
N_IN = _C_GAB[1]


def _cparams(*sem):
    return pltpu.CompilerParams(dimension_semantics=sem, vmem_limit_bytes=VMEM_LIMIT)


def _resident(shape):
    nd = len(shape)
    return pl.BlockSpec(shape, lambda *_: (0,) * nd, pipeline_mode=pl.Buffered(1))


def _dot(a, b):
    return jnp.dot(a, b, preferred_element_type=F32)


def _dot_nt(a, b):
    return lax.dot_general(a, b, (((1,), (1,)), ((), ())), preferred_element_type=F32)


def _layer_norm(xf, g, b):
    mu = jnp.mean(xf, axis=-1, keepdims=True)
    d = xf - mu
    var = jnp.mean(d * d, axis=-1, keepdims=True)
    return d * lax.rsqrt(var + LN_EPS) * g + b


def _silu(x):
    return x * jax.nn.sigmoid(x)


def _inproj_body(x_ref, w_ref, aq_ref, akv_ref, gqkv_ref, gz_ref, gab_ref):
    xb = x_ref[...].astype(BF16)

    def mm(lo, hi):
        return _dot(xb, w_ref[:, lo:hi])

    aq_ref[...] = mm(*_C_AQ).astype(BF16)
    akv_ref[...] = mm(*_C_AKV).astype(BF16)
    for c in range(3):
        lo = _C_GQKV[0] + c * GDN_W
        gqkv_ref[:, c * GDN_W:(c + 1) * GDN_W] = mm(lo, lo + GDN_W)
    gz_ref[...] = mm(*_C_GZ)
    gab_ref[...] = mm(*_C_GAB)


def _inproj(x, w_all):
    t = x.shape[0]
    row = lambda n: pl.BlockSpec((ROW_TILE, n), lambda i: (i, 0))
    return pl.pallas_call(
        _inproj_body,
        grid=(t // ROW_TILE,),
        in_specs=[row(D_MODEL), _resident((D_MODEL, N_IN))],
        out_specs=[row(ATT_Q), row(4 * LANES), row(3 * GDN_W), row(GDN_W), row(LANES)],
        out_shape=[jax.ShapeDtypeStruct((t, ATT_Q), BF16),
                   jax.ShapeDtypeStruct((t, 4 * LANES), BF16),
                   jax.ShapeDtypeStruct((t, 3 * GDN_W), F32),
                   jax.ShapeDtypeStruct((t, GDN_W), F32),
                   jax.ShapeDtypeStruct((t, LANES), F32)],
        compiler_params=_cparams("parallel"),
        name="in_proj",
    )(x, w_all)


def _swa_body(sinks_ref, q_ref, kvc_ref, kvp_ref, o_ref):
    first = pl.program_id(1) == 0
    lane = lax.broadcasted_iota(jnp.int32, (1, LANES), 1)
    q_masks = (jnp.where(lane < ATT_HEAD_DIM, ATT_SCALE, 0.0).astype(BF16),
               jnp.where(lane >= ATT_HEAD_DIM, ATT_SCALE, 0.0).astype(BF16))
    low_half = lax.broadcasted_iota(jnp.int32, (WINDOW, LANES), 1) < ATT_HEAD_DIM
    qi = lax.broadcasted_iota(jnp.int32, (WINDOW, 2 * WINDOW), 0)
    kj = lax.broadcasted_iota(jnp.int32, (WINDOW, 2 * WINDOW), 1)
    dist = qi + WINDOW - kj
    valid = (dist >= 0) & (dist < WINDOW)
    valid_first = valid & (kj >= jnp.where(first, WINDOW, 0))
    distf = dist.astype(F32)

    for blk in range(ATT_ROWS // WINDOW):
        rows = slice(blk * WINDOW, (blk + 1) * WINDOW)
        cur = kvc_ref[rows, :]
        prev = kvp_ref[...] if blk == 0 else kvc_ref[(blk - 1) * WINDOW:blk * WINDOW, :]
        band = jnp.concatenate([prev, cur], axis=0)
        vld = valid_first if blk == 0 else valid
        for pair in range(ATT_HEADS // 2):
            kvh = (2 * pair) // ATT_GROUP
            kk = band[:, kvh * LANES:(kvh + 1) * LANES]
            vv = band[:, (2 + kvh) * LANES:(3 + kvh) * LANES]
            qp = q_ref[rows, pair * LANES:(pair + 1) * LANES]
            halves = []
            for half in range(2):
                hq = 2 * pair + half
                s = _dot_nt(qp * q_masks[half], kk)
                s = jnp.where(vld, s - ALIBI_SLOPES[hq] * distf, NEG_INF)
                sink = sinks_ref[hq]
                m = jnp.maximum(jnp.max(s, axis=-1, keepdims=True), sink)
                p = jnp.exp(s - m)
                denom = jnp.sum(p, axis=-1, keepdims=True) + jnp.exp(sink - m)
                pn = (p * (1.0 / denom)).astype(BF16)
                halves.append(_dot(pn, vv))
            o_ref[rows, pair * LANES:(pair + 1) * LANES] = jnp.where(
                low_half, halves[0], halves[1]).astype(BF16)


def _swa(aq, akv, sinks, batch, seq):
    t = aq.shape[0]
    steps = seq // ATT_ROWS
    blocks_per_step = ATT_ROWS // WINDOW
    blocks_per_seq = seq // WINDOW

    def prev_map(b, j):
        return (b * blocks_per_seq + jnp.maximum(j * blocks_per_step - 1, 0), 0)

    return pl.pallas_call(
        _swa_body,
        grid=(batch, steps),
        in_specs=[pl.BlockSpec(memory_space=pltpu.SMEM),
                  pl.BlockSpec((ATT_ROWS, ATT_Q), lambda b, j: (b * steps + j, 0)),
                  pl.BlockSpec((ATT_ROWS, 4 * LANES), lambda b, j: (b * steps + j, 0)),
                  pl.BlockSpec((WINDOW, 4 * LANES), prev_map)],
        out_specs=pl.BlockSpec((ATT_ROWS, ATT_Q), lambda b, j: (b * steps + j, 0)),
        out_shape=jax.ShapeDtypeStruct((t, ATT_Q), BF16),
        compiler_params=_cparams("parallel", "parallel"),
        name="swa_attention",
    )(sinks, aq, akv, akv)


def _unit_lower_inverse_stages(mats, ii, jj, out):
    base = 16

    def same_block(size):
        return (ii // size) == (jj // size)

    def bf(ms):
        return [m.astype(BF16) for m in ms]

    eye = (ii == jj).astype(F32)
    in_base = same_block(base)
    b1 = [jnp.where(in_base, -a, 0.0) for a in mats]
    b1_b = bf(b1)
    b2_b = bf([_dot(b, b) for b in b1_b])
    yield
    t = [eye + b for b in b1]
    b4 = [_dot(b, b) for b in b2_b]
    t = [x + _dot(xb, p) for x, xb, p in zip(t, bf(t), b2_b)]
    b4_b = bf(b4)
    yield
    b8 = [_dot(b, b) for b in b4_b]
    t = [x + _dot(xb, p) for x, xb, p in zip(t, bf(t), b4_b)]
    b8_b = bf(b8)
    yield
    t = [x + _dot(xb, p) for x, xb, p in zip(t, bf(t), b8_b)]
    yield
    size = base
    while size < GDN_CHUNK:
        off_diag = same_block(2 * size) & jnp.logical_not(same_block(size))
        t_b = bf(t)
        ct_b = bf([_dot(jnp.where(off_diag, a, 0.0).astype(BF16), xb) for a, xb in zip(mats, t_b)])
        yield
        t = [x - _dot(xb, y) for x, xb, y in zip(t, t_b, ct_b)]
        yield
        size *= 2
    out.extend(t)


def _gdn_body(gp_ref, convw_ref, normg_ref, x_ref, xprev_ref, gab_ref, z_ref, o_ref,
              xs_ref, a_ref, rhs_ref, qd_ref, kdt_ref, intra_ref, cd_ref, state_ref, qkv_ref,
              *, chunks_per_seq, n_chunks):
    step = pl.program_id(0)
    c_rows = GDN_CHUNK
    heads = range(GDN_HEADS)
    head_cols = [slice(h * GDN_HEAD_DIM, (h + 1) * GDN_HEAD_DIM) for h in heads]

    @pl.when(step == 0)
    def _():
        for ref in (a_ref, rhs_ref, qd_ref, kdt_ref, intra_ref, cd_ref, state_ref):
            ref[...] = jnp.zeros_like(ref)

    ii = lax.broadcasted_iota(jnp.int32, (c_rows, c_rows), 0)
    jj = lax.broadcasted_iota(jnp.int32, (c_rows, c_rows), 1)
    causal = jj <= ii
    strict = jj < ii

    def finish():
        keep = jnp.where(lax.rem(step - 1, chunks_per_seq) == 0, 0.0, 1.0)
        t_inv = []
        yield from _unit_lower_inverse_stages([a_ref[h] for h in heads], ii, jj, t_inv)
        uw = [_dot(t.astype(BF16), rhs_ref[h]) for h, t in zip(heads, t_inv)]
        yield
        state = [state_ref[h] * keep for h in heads]
        r = [_dot(jnp.concatenate([uw[h][:, GDN_HEAD_DIM:].astype(BF16), qd_ref[:, head_cols[h]]], axis=0),
                  state[h].astype(BF16)) for h in heads]
        yield
        v_new_b = [(uw[h][:, :GDN_HEAD_DIM] - r[h][:c_rows]).astype(BF16) for h in heads]
        for h in heads:
            state_ref[h] = state[h] * cd_ref[0:1, h:h + 1] + _dot(kdt_ref[:, head_cols[h]], v_new_b[h])
        yield
        for h in heads:
            o = r[h][c_rows:] + _dot(intra_ref[:, head_cols[h]], v_new_b[h])
            o = (o * lax.rsqrt(jnp.mean(o * o, axis=-1, keepdims=True) + RMS_EPS)
                 * normg_ref[...] * _silu(z_ref[:, head_cols[h]]))
            o_ref[:, head_cols[h]] = o.astype(BF16)
        yield

    first = lax.rem(jnp.minimum(step, n_chunks - 1), chunks_per_seq) == 0
    xs_ref[0:SUBLANES, :] = jnp.where(first, 0.0, xprev_ref[...])
    xs_ref[SUBLANES:SUBLANES + c_rows, :] = x_ref[...]

    def convolve():
        for h in heads:
            for part in range(3):
                col = part * GDN_W + h * GDN_HEAD_DIM
                acc = None
                for tap in range(CONV_K):
                    lo = SUBLANES - (CONV_K - 1) + tap
                    term = convw_ref[tap:tap + 1, col:col + LANES] * xs_ref[lo:lo + c_rows, col:col + LANES]
                    acc = term if acc is None else acc + term
                y = _silu(acc)
                if part < 2:
                    y = y * lax.rsqrt(jnp.sum(y * y, axis=-1, keepdims=True) + RMS_EPS)
                qkv_ref[3 * h + part] = y
                yield

    finishing, convolving = finish(), convolve()
    n_finish, n_conv = 14, 3 * GDN_HEADS
    done = 0
    for turn, _ in enumerate(finishing):
        while done * n_finish < (turn + 1) * n_conv and next(convolving, True) is None:
            done += 1
    for _ in convolving:
        pass

    gab = gab_ref[...]
    a_log = gp_ref[0:1, :]
    dt_bias = gp_ref[1:2, :]
    g_all = -jnp.exp(a_log) * jax.nn.softplus(gab + dt_bias)
    beta_all = jax.nn.sigmoid(gab)
    ltri = causal.astype(F32)
    decay_cols = jnp.dot(ltri, g_all, precision=lax.Precision.HIGHEST, preferred_element_type=F32)
    decay_rows = lax.dot_general(g_all.T[0:SUBLANES, :], ltri, (((1,), (1,)), ((), ())),
                                 precision=lax.Precision.HIGHEST, preferred_element_type=F32)
    decay_last = decay_cols[c_rows - 1:c_rows, :]
    cd_ref[...] = jnp.broadcast_to(jnp.exp(decay_last), (SUBLANES, LANES))
    for h in heads:
        cols = head_cols[h]
        qn = qkv_ref[3 * h] * (GDN_HEAD_DIM ** -0.5)
        kn = qkv_ref[3 * h + 1]
        v = qkv_ref[3 * h + 2]
        dcol = decay_cols[:, h:h + 1]
        drow = decay_rows[h:h + 1, :]
        lmask = jnp.where(causal, jnp.exp(jnp.where(causal, dcol - drow, 0.0)), 0.0)
        e_dec = jnp.exp(dcol)
        e_rem = jnp.exp(decay_last[:, h:h + 1] - dcol)
        beta = beta_all[:, GDN_HEADS + h:GDN_HEADS + h + 1]
        kbeta = kn * beta
        kn_b = kn.astype(BF16)
        a_ref[h] = jnp.where(strict, _dot_nt(kbeta.astype(BF16), kn_b) * lmask, 0.0)
        intra_ref[:, cols] = jnp.where(causal, _dot_nt(qn.astype(BF16), kn_b) * lmask, 0.0).astype(BF16)
        rhs_ref[h] = jnp.concatenate([v * beta, kbeta * e_dec], axis=1).astype(BF16)
        qd_ref[:, cols] = (qn * e_dec).astype(BF16)
        kdt_ref[:, cols] = (kn * e_rem).T.astype(BF16)


def _gdn(gqkv, gab, gz, conv_w, gate_params, norm_g, batch, seq):
    t = gqkv.shape[0]
    chunks_per_seq = seq // GDN_CHUNK
    n_chunks = batch * chunks_per_seq
    per8 = GDN_CHUNK // SUBLANES
    cur = lambda s: jnp.minimum(s, n_chunks - 1)
    done = lambda s: jnp.maximum(s - 1, 0)
    vmem = lambda shape, dt: pltpu.VMEM(shape, dt)
    return pl.pallas_call(
        functools.partial(_gdn_body, chunks_per_seq=chunks_per_seq, n_chunks=n_chunks),
        grid=(n_chunks + 1,),
        in_specs=[_resident((SUBLANES, LANES)), _resident((CONV_K, 3 * GDN_W)),
                  _resident((1, GDN_HEAD_DIM)),
                  pl.BlockSpec((GDN_CHUNK, 3 * GDN_W), lambda s: (cur(s), 0)),
                  pl.BlockSpec((SUBLANES, 3 * GDN_W), lambda s: (jnp.maximum(cur(s) * per8 - 1, 0), 0)),
                  pl.BlockSpec((GDN_CHUNK, LANES), lambda s: (cur(s), 0)),
                  pl.BlockSpec((GDN_CHUNK, GDN_W), lambda s: (done(s), 0))],
        out_specs=pl.BlockSpec((GDN_CHUNK, GDN_W), lambda s: (done(s), 0)),
        out_shape=jax.ShapeDtypeStruct((t, GDN_W), BF16),
        scratch_shapes=[vmem((SUBLANES + GDN_CHUNK, 3 * GDN_W), F32),
                        vmem((GDN_HEADS, GDN_CHUNK, GDN_CHUNK), F32),
                        vmem((GDN_HEADS, GDN_CHUNK, 2 * GDN_HEAD_DIM), BF16),
                        vmem((GDN_CHUNK, GDN_W), BF16),
                        vmem((GDN_HEAD_DIM, GDN_HEADS * GDN_CHUNK), BF16),
                        vmem((GDN_CHUNK, GDN_HEADS * GDN_CHUNK), BF16),
                        vmem((SUBLANES, LANES), F32),
                        vmem((GDN_HEADS, GDN_HEAD_DIM, GDN_HEAD_DIM), F32),
                        vmem((3 * GDN_HEADS, GDN_CHUNK, GDN_HEAD_DIM), F32)],
        compiler_params=_cparams("arbitrary"),
        name="gdn",
    )(gate_params, conv_w, norm_g, gqkv, gqkv, gab, gz)


def _outproj_ln_body(att_ref, gdn_ref, w_ref, x_ref, g_ref, b_ref, o_ref):
    for r in range(ROW_TILE // SUB_ROWS):
        rows = slice(r * SUB_ROWS, (r + 1) * SUB_ROWS)
        y = (_dot(att_ref[rows, :], w_ref[0:ATT_Q, :])
             + _dot(gdn_ref[rows, :], w_ref[ATT_Q:ATT_Q + GDN_W, :]))
        o_ref[rows, :] = _layer_norm(ALPHA * x_ref[rows, :] + y, g_ref[...], b_ref[...])


def _outproj_ln(att, gdn, w_out, x, g, b):
    t = x.shape[0]
    row = lambda n: pl.BlockSpec((ROW_TILE, n), lambda i: (i, 0))
    return pl.pallas_call(
        _outproj_ln_body,
        grid=(t // ROW_TILE,),
        in_specs=[row(ATT_Q), row(GDN_W), _resident((D_MODEL, D_MODEL)), row(D_MODEL),
                  _resident((1, D_MODEL)), _resident((1, D_MODEL))],
        out_specs=row(D_MODEL),
        out_shape=jax.ShapeDtypeStruct((t, D_MODEL), F32),
        compiler_params=_cparams("parallel"),
        name="out_proj_ln",
    )(att, gdn, w_out, x, g, b)


def _memkv_body(m_ref, w_ref, k_ref, v_ref):
    mb = m_ref[...].astype(BF16)
    k_ref[...] = _dot(mb, w_ref[:, 0:D_MODEL]).astype(BF16)
    v_ref[...] = _dot(mb, w_ref[:, D_MODEL:2 * D_MODEL]).astype(BF16)


def _memkv(mem, w_kv):
    rows = mem.shape[0]
    tile = min(ROW_TILE, rows)
    row = pl.BlockSpec((tile, D_MODEL), lambda i: (i, 0))
    return pl.pallas_call(
        _memkv_body,
        grid=(rows // tile,),
        in_specs=[row, _resident((D_MODEL, 2 * D_MODEL))],
        out_specs=[row, row],
        out_shape=[jax.ShapeDtypeStruct((rows, D_MODEL), BF16)] * 2,
        compiler_params=_cparams("parallel"),
        name="mem_kv_proj",
    )(mem, w_kv)


def _xattn_ln_body(x_ref, wq_ref, k_ref, v_ref, wo_ref, g_ref, b_ref, o_ref):
    n_sub = ROW_TILE // SUB_ROWS
    rows = [slice(r * SUB_ROWS, (r + 1) * SUB_ROWS) for r in range(n_sub)]
    head_cols = [slice(h * X_HEAD_DIM, (h + 1) * X_HEAD_DIM) for h in range(X_HEADS)]

    def project_q(r):
        return (_dot(x_ref[rows[r], :].astype(BF16), wq_ref[...]) * X_SCALE).astype(BF16)

    def scores(q):
        return [_dot_nt(q[:, c], k_ref[:, c]) for c in head_cols]

    def softmax(s_heads):
        out = []
        for s in s_heads:
            p = jnp.exp(s - jnp.max(s, axis=-1, keepdims=True))
            out.append((p * (1.0 / jnp.sum(p, axis=-1, keepdims=True))).astype(BF16))
        return out

    def attend(p_heads):
        return jnp.concatenate([_dot(p, v_ref[:, c]).astype(BF16) for p, c in zip(p_heads, head_cols)], axis=1)

    def finish(r, heads):
        y = _dot(heads, wo_ref[...])
        o_ref[rows[r], :] = _layer_norm(ALPHA * x_ref[rows[r], :] + y, g_ref[...], b_ref[...])

    s_prev = scores(project_q(0))
    for r in range(n_sub):
        q_next = project_q(r + 1) if r + 1 < n_sub else None
        heads = attend(softmax(s_prev))
        if q_next is not None:
            s_prev = scores(q_next)
        finish(r, heads)


def _xattn_ln(x, wq, k_mem, v_mem, wo, g, b, batch, seq):
    t = x.shape[0]
    steps = seq // ROW_TILE
    mem_len = k_mem.shape[0] // batch
    row = pl.BlockSpec((ROW_TILE, D_MODEL), lambda bb, i: (bb * steps + i, 0))
    mem = pl.BlockSpec((mem_len, D_MODEL), lambda bb, i: (bb, 0))
    return pl.pallas_call(
        _xattn_ln_body,
        grid=(batch, steps),
        in_specs=[row, _resident((D_MODEL, D_MODEL)), mem, mem, _resident((D_MODEL, D_MODEL)),
                  _resident((1, D_MODEL)), _resident((1, D_MODEL))],
        out_specs=row,
        out_shape=jax.ShapeDtypeStruct((t, D_MODEL), F32),
        compiler_params=_cparams("parallel", "parallel"),
        name="mem_xattn_ln",
    )(x, wq, k_mem, v_mem, wo, g, b)


def _mlp_ln_body(x_ref, w1_ref, w2_ref, g_ref, b_ref, o_ref):
    x = x_ref[...]
    xb = x.astype(BF16)
    y = None
    for c in range(D_FF // FF_CHUNK):
        cols = slice(c * FF_CHUNK, (c + 1) * FF_CHUNK)
        hid = jnp.square(jnp.maximum(_dot(xb, w1_ref[:, cols]), 0.0)).astype(BF16)
        part = _dot(hid, w2_ref[cols, :])
        y = part if y is None else y + part
    o_ref[...] = _layer_norm(ALPHA * x + y, g_ref[...], b_ref[...])


def _mlp_ln(x, w1, w2, g, b):
    t = x.shape[0]
    row = pl.BlockSpec((ROW_TILE, D_MODEL), lambda i: (i, 0))
    return pl.pallas_call(
        _mlp_ln_body,
        grid=(t // ROW_TILE,),
        in_specs=[row, _resident((D_MODEL, D_FF)), _resident((D_FF, D_MODEL)),
                  _resident((1, D_MODEL)), _resident((1, D_MODEL))],
        out_specs=row,
        out_shape=jax.ShapeDtypeStruct((t, D_MODEL), F32),
        compiler_params=_cparams("parallel"),
        name="mlp_ln",
    )(x, w1, w2, g, b)


def _pack_w_in(w_in):
    o = 0
    aq = w_in[:, o:o + ATT_Q]; o += ATT_Q
    ak = w_in[:, o:o + ATT_KV]; o += ATT_KV
    av = w_in[:, o:o + ATT_KV]; o += ATT_KV
    gqkv = w_in[:, o:o + 3 * GDN_W]; o += 3 * GDN_W
    ga = w_in[:, o:o + GDN_HEADS]; o += GDN_HEADS
    gb = w_in[:, o:o + GDN_HEADS]; o += GDN_HEADS
    gz = w_in[:, o:o + GDN_W]
    dup = lambda m: jnp.concatenate(
        [m[:, h * ATT_HEAD_DIM:(h + 1) * ATT_HEAD_DIM] for h in range(ATT_KV_HEADS) for _ in range(2)], axis=1)
    pad = jnp.zeros((w_in.shape[0], LANES - 2 * GDN_HEADS), w_in.dtype)
    return jnp.concatenate([aq, dup(ak), dup(av), gqkv, gz, ga, gb, pad], axis=1).astype(BF16)


def _pad_lanes(v):
    return jnp.pad(v, (0, LANES - v.shape[0]))


def kernel(x, mem, w_in, conv_w, attn_sinks, a_log, dt_bias, gdn_norm_g, w_mix_out,
           wq_mem, wk_mem, wv_mem, wo_mem, w_ff1, w_ff2, ln_g, ln_b):
    batch, seq, _ = x.shape
    xt = x.reshape(batch * seq, D_MODEL)
    memt = mem.reshape(batch * mem.shape[1], D_MODEL)
    for l in range(DEPTH):
        w_all = _pack_w_in(w_in[l])
        gate_params = jnp.zeros((SUBLANES, LANES), F32)
        gate_params = gate_params.at[0].set(_pad_lanes(a_log[l])).at[1].set(_pad_lanes(dt_bias[l]))
        ln = lambda i: (ln_g[l, i].reshape(1, D_MODEL), ln_b[l, i].reshape(1, D_MODEL))

        aq, akv, gqkv, gz, gab = _inproj(xt, w_all)
        att = _swa(aq, akv, attn_sinks[l], batch, seq)
        gdn = _gdn(gqkv, gab, gz, conv_w[l], gate_params, gdn_norm_g[l].reshape(1, GDN_HEAD_DIM), batch, seq)
        xt = _outproj_ln(att, gdn, w_mix_out[l].astype(BF16), xt, *ln(0))

        w_kv = jnp.concatenate([wk_mem[l], wv_mem[l]], axis=1).astype(BF16)
        k_mem, v_mem = _memkv(memt, w_kv)
        xt = _xattn_ln(xt, wq_mem[l].astype(BF16), k_mem, v_mem, wo_mem[l].astype(BF16), *ln(1), batch, seq)

        xt = _mlp_ln(xt, w_ff1[l].astype(BF16), w_ff2[l].astype(BF16), *ln(2))
    return xt.reshape(batch, seq, D_MODEL)
```

```python
import functools

import jax
import jax.numpy as jnp
from jax import lax
from jax.experimental import pallas as pl
from jax.experimental.pallas import tpu as pltpu

F32 = jnp.float32
BF16 = jnp.bfloat16

D_MODEL = 1024
DEPTH = 2
ATT_HEADS = 8
ATT_KV_HEADS = 2
ATT_HEAD_DIM = 64
ATT_GROUP = ATT_HEADS // ATT_KV_HEADS
WINDOW = 128
ATT_Q = ATT_HEADS * ATT_HEAD_DIM
ATT_KV = ATT_KV_HEADS * ATT_HEAD_DIM
GDN_HEADS = 4
GDN_HEAD_DIM = 128
GDN_W = GDN_HEADS * GDN_HEAD_DIM
CONV_K = 4
X_HEADS = 4
X_HEAD_DIM = D_MODEL // X_HEADS
D_FF = 4 * D_MODEL
ALPHA = (2 * DEPTH) ** 0.25
LN_EPS = 1e-5
RMS_EPS = 1e-6
NEG_INF = -1e30
ATT_SCALE = ATT_HEAD_DIM ** -0.5
X_SCALE = X_HEAD_DIM ** -0.5
ALIBI_SLOPES = tuple(2.0 ** (-8.0 * (h + 1) / ATT_HEADS) for h in range(ATT_HEADS))

LANES = 128
SUBLANES = 8
ROW_TILE = 512
SUB_ROWS = 256
ATT_ROWS = 512
GDN_CHUNK = 128
FF_CHUNK = 512
VMEM_LIMIT = 56 * 1024 * 1024

_C_AQ = (0, ATT_Q)
_C_AKV = (_C_AQ[1], _C_AQ[1] + 4 * LANES)
_C_GQKV = (_C_AKV[1], _C_AKV[1] + 3 * GDN_W)
_C_GZ = (_C_GQKV[1], _C_GQKV[1] + GDN_W)
_C_GAB = (_C_GZ[1], _C_GZ[1] + LANES)
N_IN = _C_GAB[1]


def _cparams(*sem):
    return pltpu.CompilerParams(dimension_semantics=sem, vmem_limit_bytes=VMEM_LIMIT)


def _resident(shape):
    nd = len(shape)
    return pl.BlockSpec(shape, lambda *_: (0,) * nd, pipeline_mode=pl.Buffered(1))


def _dot(a, b):
    return jnp.dot(a, b, preferred_element_type=F32)


def _dot_nt(a, b):
    return lax.dot_general(a, b, (((1,), (1,)), ((), ())), preferred_element_type=F32)


def _layer_norm(xf, g, b):
    mu = jnp.mean(xf, axis=-1, keepdims=True)
    d = xf - mu
    var = jnp.mean(d * d, axis=-1, keepdims=True)
    return d * lax.rsqrt(var + LN_EPS) * g + b


def _silu(x):
    return x * jax.nn.sigmoid(x)


def _inproj_body(x_ref, w_ref, aq_ref, akv_ref, gqkv_ref, gz_ref, gab_ref):
    xb = x_ref[...].astype(BF16)

    def mm(lo, hi):
        return _dot(xb, w_ref[:, lo:hi])

    aq_ref[...] = mm(*_C_AQ).astype(BF16)
    akv_ref[...] = mm(*_C_AKV).astype(BF16)
    for c in range(3):
        lo = _C_GQKV[0] + c * GDN_W
        gqkv_ref[:, c * GDN_W:(c + 1) * GDN_W] = mm(lo, lo + GDN_W)
    gz_ref[...] = mm(*_C_GZ)
    gab_ref[...] = mm(*_C_GAB)


def _inproj(x, w_all):
    t = x.shape[0]
    row = lambda n: pl.BlockSpec((ROW_TILE, n), lambda i: (i, 0))
    return pl.pallas_call(
        _inproj_body,
        grid=(t // ROW_TILE,),
        in_specs=[row(D_MODEL), _resident((D_MODEL, N_IN))],
        out_specs=[row(ATT_Q), row(4 * LANES), row(3 * GDN_W), row(GDN_W), row(LANES)],
        out_shape=[jax.ShapeDtypeStruct((t, ATT_Q), BF16),
                   jax.ShapeDtypeStruct((t, 4 * LANES), BF16),
                   jax.ShapeDtypeStruct((t, 3 * GDN_W), F32),
                   jax.ShapeDtypeStruct((t, GDN_W), F32),
                   jax.ShapeDtypeStruct((t, LANES), F32)],
        compiler_params=_cparams("parallel"),
        name="in_proj",
    )(x, w_all)


def _swa_body(sinks_ref, q_ref, kvc_ref, kvp_ref, o_ref):
    first = pl.program_id(1) == 0
    lane = lax.broadcasted_iota(jnp.int32, (1, LANES), 1)
    q_masks = (jnp.where(lane < ATT_HEAD_DIM, ATT_SCALE, 0.0).astype(BF16),
               jnp.where(lane >= ATT_HEAD_DIM, ATT_SCALE, 0.0).astype(BF16))
    low_half = lax.broadcasted_iota(jnp.int32, (WINDOW, LANES), 1) < ATT_HEAD_DIM
    qi = lax.broadcasted_iota(jnp.int32, (WINDOW, WINDOW), 0)
    kj = lax.broadcasted_iota(jnp.int32, (WINDOW, WINDOW), 1)
    in_cur = kj <= qi
    distf = jnp.where(in_cur, qi - kj, qi + WINDOW - kj).astype(F32)
    visible_first = in_cur | (kj >= jnp.where(first, WINDOW, 0))

    n_blk = ATT_ROWS // WINDOW
    rows = [slice(blk * WINDOW, (blk + 1) * WINDOW) for blk in range(n_blk)]

    def band(blk):
        prev = kvp_ref[...] if blk == 0 else kvc_ref[rows[blk - 1], :]
        return jnp.concatenate([prev, kvc_ref[rows[blk], :]], axis=0)

    def scores(blk):
        kv = band(blk)
        out = []
        for hq in range(ATT_HEADS):
            kk = kv[:, (hq // ATT_GROUP) * LANES:(hq // ATT_GROUP + 1) * LANES]
            qp = q_ref[rows[blk], (hq // 2) * LANES:(hq // 2 + 1) * LANES]
            out.append(_dot_nt(qp * q_masks[hq % 2], kk))
        return out

    def attend(blk, s2_heads):
        kv = band(blk)
        halves = []
        for hq, s2 in enumerate(s2_heads):
            vv = kv[:, (2 + hq // ATT_GROUP) * LANES:(3 + hq // ATT_GROUP) * LANES]
            s = jnp.where(in_cur, s2[:, WINDOW:], s2[:, :WINDOW]) - ALIBI_SLOPES[hq] * distf
            if blk == 0:
                s = jnp.where(visible_first, s, NEG_INF)
            sink = sinks_ref[hq]
            m = jnp.maximum(jnp.max(s, axis=-1, keepdims=True), sink)
            p = jnp.exp(s - m)
            denom = jnp.sum(p, axis=-1, keepdims=True) + jnp.exp(sink - m)
            pn = p * (1.0 / denom)
            p2 = jnp.concatenate([jnp.where(in_cur, 0.0, pn), jnp.where(in_cur, pn, 0.0)], axis=1)
            halves.append(_dot(p2.astype(BF16), vv))
        for pair in range(ATT_HEADS // 2):
            o_ref[rows[blk], pair * LANES:(pair + 1) * LANES] = jnp.where(
                low_half, halves[2 * pair], halves[2 * pair + 1]).astype(BF16)

    s2_heads = scores(0)
    for blk in range(n_blk):
        s2_next = scores(blk + 1) if blk + 1 < n_blk else None
        attend(blk, s2_heads)
        s2_heads = s2_next


def _swa(aq, akv, sinks, batch, seq):
    t = aq.shape[0]
    steps = seq // ATT_ROWS
    blocks_per_step = ATT_ROWS // WINDOW
    blocks_per_seq = seq // WINDOW

    def prev_map(b, j):
        return (b * blocks_per_seq + jnp.maximum(j * blocks_per_step - 1, 0), 0)

    return pl.pallas_call(
        _swa_body,
        grid=(batch, steps),
        in_specs=[pl.BlockSpec(memory_space=pltpu.SMEM),
                  pl.BlockSpec((ATT_ROWS, ATT_Q), lambda b, j: (b * steps + j, 0)),
                  pl.BlockSpec((ATT_ROWS, 4 * LANES), lambda b, j: (b * steps + j, 0)),
                  pl.BlockSpec((WINDOW, 4 * LANES), prev_map)],
        out_specs=pl.BlockSpec((ATT_ROWS, ATT_Q), lambda b, j: (b * steps + j, 0)),
        out_shape=jax.ShapeDtypeStruct((t, ATT_Q), BF16),
        compiler_params=_cparams("parallel", "parallel"),
        name="swa_attention",
    )(sinks, aq, akv, akv)


def _unit_lower_inverse_stages(mats, ii, jj, out):
    base = 16

    def same_block(size):
        return (ii // size) == (jj // size)

    def bf(ms):
        return [m.astype(BF16) for m in ms]

    eye = (ii == jj).astype(F32)
    in_base = same_block(base)
    b1 = [jnp.where(in_base, -a, 0.0) for a in mats]
    b1_b = bf(b1)
    b2_b = bf([_dot(b, b) for b in b1_b])
    yield
    t = [eye + b for b in b1]
    b4 = [_dot(b, b) for b in b2_b]
    t = [x + _dot(xb, p) for x, xb, p in zip(t, bf(t), b2_b)]
    b4_b = bf(b4)
    yield
    b8 = [_dot(b, b) for b in b4_b]
    t = [x + _dot(xb, p) for x, xb, p in zip(t, bf(t), b4_b)]
    b8_b = bf(b8)
    yield
    t = [x + _dot(xb, p) for x, xb, p in zip(t, bf(t), b8_b)]
    yield
    size = base
    while size < GDN_CHUNK:
        off_diag = same_block(2 * size) & jnp.logical_not(same_block(size))
        t_b = bf(t)
        ct_b = bf([_dot(jnp.where(off_diag, a, 0.0).astype(BF16), xb) for a, xb in zip(mats, t_b)])
        yield
        t = [x - _dot(xb, y) for x, xb, y in zip(t, t_b, ct_b)]
        yield
        size *= 2
    out.extend(t)


def _gdn_body(gp_ref, convw_ref, normg_ref, x_ref, xprev_ref, gab_ref, z_ref, o_ref,
              xs_ref, a_ref, rhs_ref, qd_ref, kdt_ref, intra_ref, cd_ref, state_ref, qkv_ref,
              *, chunks_per_seq, n_chunks):
    step = pl.program_id(0)
    c_rows = GDN_CHUNK
    heads = range(GDN_HEADS)
    head_cols = [slice(h * GDN_HEAD_DIM, (h + 1) * GDN_HEAD_DIM) for h in heads]

    @pl.when(step == 0)
    def _():
        for ref in (a_ref, rhs_ref, qd_ref, kdt_ref, intra_ref, cd_ref, state_ref):
            ref[...] = jnp.zeros_like(ref)

    ii = lax.broadcasted_iota(jnp.int32, (c_rows, c_rows), 0)
    jj = lax.broadcasted_iota(jnp.int32, (c_rows, c_rows), 1)
    causal = jj <= ii
    strict = jj < ii

    def finish():
        keep = jnp.where(lax.rem(step - 1, chunks_per_seq) == 0, 0.0, 1.0)
        t_inv = []
        yield from _unit_lower_inverse_stages([a_ref[h] for h in heads], ii, jj, t_inv)
        uw = [_dot(t.astype(BF16), rhs_ref[h]) for h, t in zip(heads, t_inv)]
        yield
        state = [state_ref[h] * keep for h in heads]
        r = [_dot(jnp.concatenate([uw[h][:, GDN_HEAD_DIM:].astype(BF16), qd_ref[:, head_cols[h]]], axis=0),
                  state[h].astype(BF16)) for h in heads]
        yield
        v_new_b = [(uw[h][:, :GDN_HEAD_DIM] - r[h][:c_rows]).astype(BF16) for h in heads]
        for h in heads:
            state_ref[h] = state[h] * cd_ref[0:1, h:h + 1] + _dot(kdt_ref[:, head_cols[h]], v_new_b[h])
        yield
        for h in heads:
            o = r[h][c_rows:] + _dot(intra_ref[:, head_cols[h]], v_new_b[h])
            o = (o * lax.rsqrt(jnp.mean(o * o, axis=-1, keepdims=True) + RMS_EPS)
                 * normg_ref[...] * _silu(z_ref[:, head_cols[h]]))
            o_ref[:, head_cols[h]] = o.astype(BF16)
        yield

    first = lax.rem(jnp.minimum(step, n_chunks - 1), chunks_per_seq) == 0
    xs_ref[0:SUBLANES, :] = jnp.where(first, 0.0, xprev_ref[...])
    xs_ref[SUBLANES:SUBLANES + c_rows, :] = x_ref[...]

    def convolve():
        for h in heads:
            for part in range(3):
                col = part * GDN_W + h * GDN_HEAD_DIM
                acc = None
                for tap in range(CONV_K):
                    lo = SUBLANES - (CONV_K - 1) + tap
                    term = convw_ref[tap:tap + 1, col:col + LANES] * xs_ref[lo:lo + c_rows, col:col + LANES]
                    acc = term if acc is None else acc + term
                y = _silu(acc)
                if part < 2:
                    y = y * lax.rsqrt(jnp.sum(y * y, axis=-1, keepdims=True) + RMS_EPS)
                qkv_ref[3 * h + part] = y
                yield

    finishing, convolving = finish(), convolve()
    n_finish, n_conv = 14, 3 * GDN_HEADS
    done = 0
    for turn, _ in enumerate(finishing):
        while done * n_finish < (turn + 1) * n_conv and next(convolving, True) is None:
            done += 1
    for _ in convolving:
        pass

    gab = gab_ref[...]
    a_log = gp_ref[0:1, :]
    dt_bias = gp_ref[1:2, :]
    g_all = -jnp.exp(a_log) * jax.nn.softplus(gab + dt_bias)
    beta_all = jax.nn.sigmoid(gab)
    ltri = causal.astype(F32)
    decay_cols = jnp.dot(ltri, g_all, precision=lax.Precision.HIGHEST, preferred_element_type=F32)
    decay_rows = lax.dot_general(g_all.T[0:SUBLANES, :], ltri, (((1,), (1,)), ((), ())),
                                 precision=lax.Precision.HIGHEST, preferred_element_type=F32)
    decay_last = decay_cols[c_rows - 1:c_rows, :]
    cd_ref[...] = jnp.broadcast_to(jnp.exp(decay_last), (SUBLANES, LANES))
    for h in heads:
        cols = head_cols[h]
        qn = qkv_ref[3 * h] * (GDN_HEAD_DIM ** -0.5)
        kn = qkv_ref[3 * h + 1]
        v = qkv_ref[3 * h + 2]
        dcol = decay_cols[:, h:h + 1]
        drow = decay_rows[h:h + 1, :]
        lmask = jnp.where(causal, jnp.exp(jnp.where(causal, dcol - drow, 0.0)), 0.0)
        e_dec = jnp.exp(dcol)
        e_rem = jnp.exp(decay_last[:, h:h + 1] - dcol)
        beta = beta_all[:, GDN_HEADS + h:GDN_HEADS + h + 1]
        kbeta = kn * beta
        kn_b = kn.astype(BF16)
        a_ref[h] = jnp.where(strict, _dot_nt(kbeta.astype(BF16), kn_b) * lmask, 0.0)
        intra_ref[:, cols] = jnp.where(causal, _dot_nt(qn.astype(BF16), kn_b) * lmask, 0.0).astype(BF16)
        rhs_ref[h] = jnp.concatenate([v * beta, kbeta * e_dec], axis=1).astype(BF16)
        qd_ref[:, cols] = (qn * e_dec).astype(BF16)
        kdt_ref[:, cols] = (kn * e_rem).T.astype(BF16)


def _gdn(gqkv, gab, gz, conv_w, gate_params, norm_g, batch, seq):
    t = gqkv.shape[0]
    chunks_per_seq = seq // GDN_CHUNK
    n_chunks = batch * chunks_per_seq
    per8 = GDN_CHUNK // SUBLANES
    cur = lambda s: jnp.minimum(s, n_chunks - 1)
    done = lambda s: jnp.maximum(s - 1, 0)
    vmem = lambda shape, dt: pltpu.VMEM(shape, dt)
    return pl.pallas_call(
        functools.partial(_gdn_body, chunks_per_seq=chunks_per_seq, n_chunks=n_chunks),
        grid=(n_chunks + 1,),
        in_specs=[_resident((SUBLANES, LANES)), _resident((CONV_K, 3 * GDN_W)),
                  _resident((1, GDN_HEAD_DIM)),
                  pl.BlockSpec((GDN_CHUNK, 3 * GDN_W), lambda s: (cur(s), 0)),
                  pl.BlockSpec((SUBLANES, 3 * GDN_W), lambda s: (jnp.maximum(cur(s) * per8 - 1, 0), 0)),
                  pl.BlockSpec((GDN_CHUNK, LANES), lambda s: (cur(s), 0)),
                  pl.BlockSpec((GDN_CHUNK, GDN_W), lambda s: (done(s), 0))],
        out_specs=pl.BlockSpec((GDN_CHUNK, GDN_W), lambda s: (done(s), 0)),
        out_shape=jax.ShapeDtypeStruct((t, GDN_W), BF16),
        scratch_shapes=[vmem((SUBLANES + GDN_CHUNK, 3 * GDN_W), F32),
                        vmem((GDN_HEADS, GDN_CHUNK, GDN_CHUNK), F32),
                        vmem((GDN_HEADS, GDN_CHUNK, 2 * GDN_HEAD_DIM), BF16),
                        vmem((GDN_CHUNK, GDN_W), BF16),
                        vmem((GDN_HEAD_DIM, GDN_HEADS * GDN_CHUNK), BF16),
                        vmem((GDN_CHUNK, GDN_HEADS * GDN_CHUNK), BF16),
                        vmem((SUBLANES, LANES), F32),
                        vmem((GDN_HEADS, GDN_HEAD_DIM, GDN_HEAD_DIM), F32),
                        vmem((3 * GDN_HEADS, GDN_CHUNK, GDN_HEAD_DIM), F32)],
        compiler_params=_cparams("arbitrary"),
        name="gdn",
    )(gate_params, conv_w, norm_g, gqkv, gqkv, gab, gz)


def _memkv_body(m_ref, w_ref, k_ref, v_ref):
    mb = m_ref[...].astype(BF16)
    k_ref[...] = _dot(mb, w_ref[:, 0:D_MODEL]).astype(BF16)
    v_ref[...] = _dot(mb, w_ref[:, D_MODEL:2 * D_MODEL]).astype(BF16)


def _memkv(mem, w_kv):
    rows = mem.shape[0]
    tile = min(ROW_TILE, rows)
    row = pl.BlockSpec((tile, D_MODEL), lambda i: (i, 0))
    return pl.pallas_call(
        _memkv_body,
        grid=(rows // tile,),
        in_specs=[row, _resident((D_MODEL, 2 * D_MODEL))],
        out_specs=[row, row],
        out_shape=[jax.ShapeDtypeStruct((rows, D_MODEL), BF16)] * 2,
        compiler_params=_cparams("parallel"),
        name="mem_kv_proj",
    )(mem, w_kv)


def _mix_xattn_body(att_ref, gdn_ref, x_ref, wmix_ref, wq_ref, k_ref, v_ref, wo_ref,
                    g0_ref, b0_ref, g1_ref, b1_ref, o_ref):
    n_sub = ROW_TILE // SUB_ROWS
    rows = [slice(r * SUB_ROWS, (r + 1) * SUB_ROWS) for r in range(n_sub)]
    head_cols = [slice(h * X_HEAD_DIM, (h + 1) * X_HEAD_DIM) for h in range(X_HEADS)]

    def mix(r):
        y = (_dot(att_ref[rows[r], :], wmix_ref[0:ATT_Q, :])
             + _dot(gdn_ref[rows[r], :], wmix_ref[ATT_Q:ATT_Q + GDN_W, :]))
        return _layer_norm(ALPHA * x_ref[rows[r], :] + y, g0_ref[...], b0_ref[...])

    def project_q(x1):
        return (_dot(x1.astype(BF16), wq_ref[...]) * X_SCALE).astype(BF16)

    def scores(q):
        return [_dot_nt(q[:, c], k_ref[:, c]) for c in head_cols]

    def softmax(s_heads):
        out = []
        for s in s_heads:
            p = jnp.exp(s - jnp.max(s, axis=-1, keepdims=True))
            out.append((p * (1.0 / jnp.sum(p, axis=-1, keepdims=True))).astype(BF16))
        return out

    def attend(p_heads):
        return jnp.concatenate([_dot(p, v_ref[:, c]).astype(BF16) for p, c in zip(p_heads, head_cols)], axis=1)

    def finish(r, x1, heads):
        y = _dot(heads, wo_ref[...])
        o_ref[rows[r], :] = _layer_norm(ALPHA * x1 + y, g1_ref[...], b1_ref[...])

    x1 = [mix(r) for r in range(n_sub)]
    s_prev = scores(project_q(x1[0]))
    for r in range(n_sub):
        q_next = project_q(x1[r + 1]) if r + 1 < n_sub else None
        heads = attend(softmax(s_prev))
        if q_next is not None:
            s_prev = scores(q_next)
        finish(r, x1[r], heads)


def _mix_xattn(att, gdn, x, w_mix, wq, k_mem, v_mem, wo, ln0, ln1, batch, seq):
    t = x.shape[0]
    steps = seq // ROW_TILE
    mem_len = k_mem.shape[0] // batch
    row = lambda n: pl.BlockSpec((ROW_TILE, n), lambda bb, i: (bb * steps + i, 0))
    mem = pl.BlockSpec((mem_len, D_MODEL), lambda bb, i: (bb, 0))
    weight = _resident((D_MODEL, D_MODEL))
    vec = _resident((1, D_MODEL))
    return pl.pallas_call(
        _mix_xattn_body,
        grid=(batch, steps),
        in_specs=[row(ATT_Q), row(GDN_W), row(D_MODEL), weight, weight, mem, mem, weight, vec, vec, vec, vec],
        out_specs=row(D_MODEL),
        out_shape=jax.ShapeDtypeStruct((t, D_MODEL), F32),
        compiler_params=_cparams("parallel", "parallel"),
        name="mix_xattn_ln",
    )(att, gdn, x, w_mix, wq, k_mem, v_mem, wo, *ln0, *ln1)


def _mlp_ln_body(x_ref, w1_ref, w2_ref, g_ref, b_ref, o_ref):
    x = x_ref[...]
    xb = x.astype(BF16)
    y = None
    for c in range(D_FF // FF_CHUNK):
        cols = slice(c * FF_CHUNK, (c + 1) * FF_CHUNK)
        hid = jnp.square(jnp.maximum(_dot(xb, w1_ref[:, cols]), 0.0)).astype(BF16)
        part = _dot(hid, w2_ref[cols, :])
        y = part if y is None else y + part
    o_ref[...] = _layer_norm(ALPHA * x + y, g_ref[...], b_ref[...])


def _mlp_ln(x, w1, w2, g, b):
    t = x.shape[0]
    row = pl.BlockSpec((ROW_TILE, D_MODEL), lambda i: (i, 0))
    return pl.pallas_call(
        _mlp_ln_body,
        grid=(t // ROW_TILE,),
        in_specs=[row, _resident((D_MODEL, D_FF)), _resident((D_FF, D_MODEL)),
                  _resident((1, D_MODEL)), _resident((1, D_MODEL))],
        out_specs=row,
        out_shape=jax.ShapeDtypeStruct((t, D_MODEL), F32),
        compiler_params=_cparams("parallel"),
        name="mlp_ln",
    )(x, w1, w2, g, b)


def _pack_w_in(w_in):
    o = 0
    aq = w_in[:, o:o + ATT_Q]; o += ATT_Q
    ak = w_in[:, o:o + ATT_KV]; o += ATT_KV
    av = w_in[:, o:o + ATT_KV]; o += ATT_KV
    gqkv = w_in[:, o:o + 3 * GDN_W]; o += 3 * GDN_W
    ga = w_in[:, o:o + GDN_HEADS]; o += GDN_HEADS
    gb = w_in[:, o:o + GDN_HEADS]; o += GDN_HEADS
    gz = w_in[:, o:o + GDN_W]
    dup = lambda m: jnp.concatenate(
        [m[:, h * ATT_HEAD_DIM:(h + 1) * ATT_HEAD_DIM] for h in range(ATT_KV_HEADS) for _ in range(2)], axis=1)
    pad = jnp.zeros((w_in.shape[0], LANES - 2 * GDN_HEADS), w_in.dtype)
    return jnp.concatenate([aq, dup(ak), dup(av), gqkv, gz, ga, gb, pad], axis=1).astype(BF16)


def _pad_lanes(v):
    return jnp.pad(v, (0, LANES - v.shape[0]))


def kernel(x, mem, w_in, conv_w, attn_sinks, a_log, dt_bias, gdn_norm_g, w_mix_out,
           wq_mem, wk_mem, wv_mem, wo_mem, w_ff1, w_ff2, ln_g, ln_b):
    batch, seq, _ = x.shape
    xt = x.reshape(batch * seq, D_MODEL)
    memt = mem.reshape(batch * mem.shape[1], D_MODEL)
    for l in range(DEPTH):
        w_all = _pack_w_in(w_in[l])
        gate_params = jnp.zeros((SUBLANES, LANES), F32)
        gate_params = gate_params.at[0].set(_pad_lanes(a_log[l])).at[1].set(_pad_lanes(dt_bias[l]))
        ln = lambda i: (ln_g[l, i].reshape(1, D_MODEL), ln_b[l, i].reshape(1, D_MODEL))

        aq, akv, gqkv, gz, gab = _inproj(xt, w_all)
        att = _swa(aq, akv, attn_sinks[l], batch, seq)
        gdn = _gdn(gqkv, gab, gz, conv_w[l], gate_params, gdn_norm_g[l].reshape(1, GDN_HEAD_DIM), batch, seq)
        w_kv = jnp.concatenate([wk_mem[l], wv_mem[l]], axis=1).astype(BF16)
        k_mem, v_mem = _memkv(memt, w_kv)
        xt = _mix_xattn(att, gdn, xt, w_mix_out[l].astype(BF16), wq_mem[l].astype(BF16), k_mem, v_mem,
                        wo_mem[l].astype(BF16), ln(0), ln(1), batch, seq)

        xt = _mlp_ln(xt, w_ff1[l].astype(BF16), w_ff2[l].astype(BF16), *ln(2))
    return xt.reshape(batch, seq, D_MODEL)
```

```python
import functools

import jax
import jax.numpy as jnp
from jax import lax
from jax.experimental import pallas as pl
from jax.experimental.pallas import tpu as pltpu

F32 = jnp.float32
BF16 = jnp.bfloat16

D_MODEL = 1024
DEPTH = 2
ATT_HEADS = 8
ATT_KV_HEADS = 2
ATT_HEAD_DIM = 64
ATT_GROUP = ATT_HEADS // ATT_KV_HEADS
WINDOW = 128
ATT_Q = ATT_HEADS * ATT_HEAD_DIM
ATT_KV = ATT_KV_HEADS * ATT_HEAD_DIM
GDN_HEADS = 4
GDN_HEAD_DIM = 128
GDN_W = GDN_HEADS * GDN_HEAD_DIM
CONV_K = 4
X_HEADS = 4
X_HEAD_DIM = D_MODEL // X_HEADS
D_FF = 4 * D_MODEL
ALPHA = (2 * DEPTH) ** 0.25
LN_EPS = 1e-5
RMS_EPS = 1e-6
NEG_INF = -1e30
ATT_SCALE = ATT_HEAD_DIM ** -0.5
X_SCALE = X_HEAD_DIM ** -0.5
ALIBI_SLOPES = tuple(2.0 ** (-8.0 * (h + 1) / ATT_HEADS) for h in range(ATT_HEADS))

LANES = 128
SUBLANES = 8
ROW_TILE = 512
SUB_ROWS = 256
ATT_ROWS = 512
GDN_CHUNK = 128
GDN_STEP_CHUNKS = 2
FF_CHUNK = 512
VMEM_LIMIT = 56 * 1024 * 1024

_C_AQ = (0, ATT_Q)
_C_AKV = (_C_AQ[1], _C_AQ[1] + 4 * LANES)
_C_GQKV = (_C_AKV[1], _C_AKV[1] + 3 * GDN_W)
_C_GZ = (_C_GQKV[1], _C_GQKV[1] + GDN_W)
_C_GAB = (_C_GZ[1], _C_GZ[1] + LANES)
N_IN = _C_GAB[1]


def _cparams(*sem):
    return pltpu.CompilerParams(dimension_semantics=sem, vmem_limit_bytes=VMEM_LIMIT)


def _resident(shape):
    nd = len(shape)
    return pl.BlockSpec(shape, lambda *_: (0,) * nd, pipeline_mode=pl.Buffered(1))


def _dot(a, b):
    return jnp.dot(a, b, preferred_element_type=F32)


def _dot_nt(a, b):
    return lax.dot_general(a, b, (((1,), (1,)), ((), ())), preferred_element_type=F32)


def _layer_norm(xf, g, b):
    mu = jnp.mean(xf, axis=-1, keepdims=True)
    d = xf - mu
    var = jnp.mean(d * d, axis=-1, keepdims=True)
    return d * lax.rsqrt(var + LN_EPS) * g + b


def _silu(x):
    return x * jax.nn.sigmoid(x)


def _inproj_body(x_ref, w_ref, aq_ref, akv_ref, gqkv_ref, gz_ref, gab_ref):
    xb = x_ref[...].astype(BF16)

    def mm(lo, hi):
        return _dot(xb, w_ref[:, lo:hi])

    aq_ref[...] = mm(*_C_AQ).astype(BF16)
    akv_ref[...] = mm(*_C_AKV).astype(BF16)
    for c in range(3):
        lo = _C_GQKV[0] + c * GDN_W
        gqkv_ref[:, c * GDN_W:(c + 1) * GDN_W] = mm(lo, lo + GDN_W)
    gz_ref[...] = mm(*_C_GZ)
    gab_ref[...] = mm(*_C_GAB)


def _inproj(x, w_all):
    t = x.shape[0]
    row = lambda n: pl.BlockSpec((ROW_TILE, n), lambda i: (i, 0))
    return pl.pallas_call(
        _inproj_body,
        grid=(t // ROW_TILE,),
        in_specs=[row(D_MODEL), _resident((D_MODEL, N_IN))],
        out_specs=[row(ATT_Q), row(4 * LANES), row(3 * GDN_W), row(GDN_W), row(LANES)],
        out_shape=[jax.ShapeDtypeStruct((t, ATT_Q), BF16),
                   jax.ShapeDtypeStruct((t, 4 * LANES), BF16),
                   jax.ShapeDtypeStruct((t, 3 * GDN_W), F32),
                   jax.ShapeDtypeStruct((t, GDN_W), F32),
                   jax.ShapeDtypeStruct((t, LANES), F32)],
        compiler_params=_cparams("parallel"),
        name="in_proj",
    )(x, w_all)


def _swa_body(sinks_ref, q_ref, kvc_ref, kvp_ref, o_ref):
    first = pl.program_id(1) == 0
    lane = lax.broadcasted_iota(jnp.int32, (1, LANES), 1)
    q_masks = (jnp.where(lane < ATT_HEAD_DIM, ATT_SCALE, 0.0).astype(BF16),
               jnp.where(lane >= ATT_HEAD_DIM, ATT_SCALE, 0.0).astype(BF16))
    low_half = lax.broadcasted_iota(jnp.int32, (WINDOW, LANES), 1) < ATT_HEAD_DIM
    qi = lax.broadcasted_iota(jnp.int32, (WINDOW, WINDOW), 0)
    kj = lax.broadcasted_iota(jnp.int32, (WINDOW, WINDOW), 1)
    in_cur = kj <= qi
    distf = jnp.where(in_cur, qi - kj, qi + WINDOW - kj).astype(F32)
    visible_first = in_cur | (kj >= jnp.where(first, WINDOW, 0))

    n_blk = ATT_ROWS // WINDOW
    rows = [slice(blk * WINDOW, (blk + 1) * WINDOW) for blk in range(n_blk)]

    def band(blk):
        prev = kvp_ref[...] if blk == 0 else kvc_ref[rows[blk - 1], :]
        return jnp.concatenate([prev, kvc_ref[rows[blk], :]], axis=0)

    def scores(blk):
        kv = band(blk)
        out = []
        for hq in range(ATT_HEADS):
            kk = kv[:, (hq // ATT_GROUP) * LANES:(hq // ATT_GROUP + 1) * LANES]
            qp = q_ref[rows[blk], (hq // 2) * LANES:(hq // 2 + 1) * LANES]
            out.append(_dot_nt(qp * q_masks[hq % 2], kk))
        return out

    def attend(blk, s2_heads):
        kv = band(blk)
        halves = []
        for hq, s2 in enumerate(s2_heads):
            vv = kv[:, (2 + hq // ATT_GROUP) * LANES:(3 + hq // ATT_GROUP) * LANES]
            s = jnp.where(in_cur, s2[:, WINDOW:], s2[:, :WINDOW]) - ALIBI_SLOPES[hq] * distf
            if blk == 0:
                s = jnp.where(visible_first, s, NEG_INF)
            sink = sinks_ref[hq]
            m = jnp.maximum(jnp.max(s, axis=-1, keepdims=True), sink)
            p = jnp.exp(s - m)
            denom = jnp.sum(p, axis=-1, keepdims=True) + jnp.exp(sink - m)
            pn = p * (1.0 / denom)
            p2 = jnp.concatenate([jnp.where(in_cur, 0.0, pn), jnp.where(in_cur, pn, 0.0)], axis=1)
            halves.append(_dot(p2.astype(BF16), vv))
        for pair in range(ATT_HEADS // 2):
            o_ref[rows[blk], pair * LANES:(pair + 1) * LANES] = jnp.where(
                low_half, halves[2 * pair], halves[2 * pair + 1]).astype(BF16)

    s2_heads = scores(0)
    for blk in range(n_blk):
        s2_next = scores(blk + 1) if blk + 1 < n_blk else None
        attend(blk, s2_heads)
        s2_heads = s2_next


def _swa(aq, akv, sinks, batch, seq):
    t = aq.shape[0]
    steps = seq // ATT_ROWS
    blocks_per_step = ATT_ROWS // WINDOW
    blocks_per_seq = seq // WINDOW

    def prev_map(b, j):
        return (b * blocks_per_seq + jnp.maximum(j * blocks_per_step - 1, 0), 0)

    return pl.pallas_call(
        _swa_body,
        grid=(batch, steps),
        in_specs=[pl.BlockSpec(memory_space=pltpu.SMEM),
                  pl.BlockSpec((ATT_ROWS, ATT_Q), lambda b, j: (b * steps + j, 0)),
                  pl.BlockSpec((ATT_ROWS, 4 * LANES), lambda b, j: (b * steps + j, 0)),
                  pl.BlockSpec((WINDOW, 4 * LANES), prev_map)],
        out_specs=pl.BlockSpec((ATT_ROWS, ATT_Q), lambda b, j: (b * steps + j, 0)),
        out_shape=jax.ShapeDtypeStruct((t, ATT_Q), BF16),
        compiler_params=_cparams("parallel", "parallel"),
        name="swa_attention",
    )(sinks, aq, akv, akv)


def _unit_lower_inverse_stages(mats, ii, jj, out):
    base = 16

    def same_block(size):
        return (ii // size) == (jj // size)

    def bf(ms):
        return [m.astype(BF16) for m in ms]

    eye = (ii == jj).astype(F32)
    in_base = same_block(base)
    b1 = [jnp.where(in_base, -a, 0.0) for a in mats]
    b1_b = bf(b1)
    b2_b = bf([_dot(b, b) for b in b1_b])
    yield
    t = [eye + b for b in b1]
    b4 = [_dot(b, b) for b in b2_b]
    t = [x + _dot(xb, p) for x, xb, p in zip(t, bf(t), b2_b)]
    b4_b = bf(b4)
    yield
    b8 = [_dot(b, b) for b in b4_b]
    t = [x + _dot(xb, p) for x, xb, p in zip(t, bf(t), b4_b)]
    b8_b = bf(b8)
    yield
    t = [x + _dot(xb, p) for x, xb, p in zip(t, bf(t), b8_b)]
    yield
    size = base
    while size < GDN_CHUNK:
        off_diag = same_block(2 * size) & jnp.logical_not(same_block(size))
        t_b = bf(t)
        ct_b = bf([_dot(jnp.where(off_diag, a, 0.0).astype(BF16), xb) for a, xb in zip(mats, t_b)])
        yield
        t = [x - _dot(xb, y) for x, xb, y in zip(t, t_b, ct_b)]
        yield
        size *= 2
    out.extend(t)


def _gdn_body(gp_ref, convw_ref, normg_ref, x_ref, xprev_ref, gab_ref, z_ref, o_ref,
              xs_ref, a_ref, rhs_ref, qd_ref, kdt_ref, intra_ref, cd_ref, state_ref, qkv_ref,
              *, groups_per_seq, n_groups):
    step = pl.program_id(0)
    c_rows = GDN_CHUNK
    subs = range(GDN_STEP_CHUNKS)
    heads = range(GDN_HEADS)
    pairs = [(j, h) for j in subs for h in heads]
    slot = lambda j, h: j * GDN_HEADS + h
    head_cols = [slice(h * GDN_HEAD_DIM, (h + 1) * GDN_HEAD_DIM) for h in heads]
    chunk_rows = [slice(j * c_rows, (j + 1) * c_rows) for j in subs]

    @pl.when(step == 0)
    def _():
        for ref in (a_ref, rhs_ref, qd_ref, kdt_ref, intra_ref, cd_ref, state_ref):
            ref[...] = jnp.zeros_like(ref)

    ii = lax.broadcasted_iota(jnp.int32, (c_rows, c_rows), 0)
    jj = lax.broadcasted_iota(jnp.int32, (c_rows, c_rows), 1)
    causal = jj <= ii
    strict = jj < ii

    def finish():
        t_inv = []
        yield from _unit_lower_inverse_stages([a_ref[slot(j, h)] for j, h in pairs], ii, jj, t_inv)
        uw = [_dot(t.astype(BF16), rhs_ref[slot(j, h)]) for (j, h), t in zip(pairs, t_inv)]
        yield
        keep = jnp.where(lax.rem(step - 1, groups_per_seq) == 0, 0.0, 1.0)
        state = [state_ref[h] * keep for h in heads]
        for j in subs:
            r = [_dot(jnp.concatenate([uw[slot(j, h)][:, GDN_HEAD_DIM:].astype(BF16),
                                       qd_ref[j, :, head_cols[h]]], axis=0),
                      state[h].astype(BF16)) for h in heads]
            yield
            v_new_b = [(uw[slot(j, h)][:, :GDN_HEAD_DIM] - r[h][:c_rows]).astype(BF16) for h in heads]
            state = [state[h] * cd_ref[j, 0:1, h:h + 1] + _dot(kdt_ref[j, :, head_cols[h]], v_new_b[h])
                     for h in heads]
            yield
            for h in heads:
                o = r[h][c_rows:] + _dot(intra_ref[j, :, head_cols[h]], v_new_b[h])
                o = (o * lax.rsqrt(jnp.mean(o * o, axis=-1, keepdims=True) + RMS_EPS)
                     * normg_ref[...] * _silu(z_ref[chunk_rows[j], head_cols[h]]))
                o_ref[chunk_rows[j], head_cols[h]] = o.astype(BF16)
            yield
        for h in heads:
            state_ref[h] = state[h]

    first = lax.rem(jnp.minimum(step, n_groups - 1), groups_per_seq) == 0
    xs_ref[0:SUBLANES, :] = jnp.where(first, 0.0, xprev_ref[...])
    xs_ref[SUBLANES:SUBLANES + GDN_STEP_CHUNKS * c_rows, :] = x_ref[...]

    def convolve():
        for j, h in pairs:
            for part in range(3):
                col = part * GDN_W + h * GDN_HEAD_DIM
                acc = None
                for tap in range(CONV_K):
                    lo = SUBLANES + j * c_rows - (CONV_K - 1) + tap
                    term = convw_ref[tap:tap + 1, col:col + LANES] * xs_ref[lo:lo + c_rows, col:col + LANES]
                    acc = term if acc is None else acc + term
                y = _silu(acc)
                if part < 2:
                    y = y * lax.rsqrt(jnp.sum(y * y, axis=-1, keepdims=True) + RMS_EPS)
                qkv_ref[3 * slot(j, h) + part] = y
                yield

    finishing, convolving = finish(), convolve()
    n_finish, n_conv = 11 + 3 * GDN_STEP_CHUNKS, 3 * len(pairs)
    done = 0
    for turn, _ in enumerate(finishing):
        while done * n_finish < (turn + 1) * n_conv and next(convolving, True) is None:
            done += 1
    for _ in convolving:
        pass

    a_log = gp_ref[0:1, :]
    dt_bias = gp_ref[1:2, :]
    ltri = causal.astype(F32)
    for j in subs:
        gab = gab_ref[chunk_rows[j], :]
        g_all = -jnp.exp(a_log) * jax.nn.softplus(gab + dt_bias)
        beta_all = jax.nn.sigmoid(gab)
        decay_cols = jnp.dot(ltri, g_all, precision=lax.Precision.HIGHEST, preferred_element_type=F32)
        decay_rows = lax.dot_general(g_all.T[0:SUBLANES, :], ltri, (((1,), (1,)), ((), ())),
                                     precision=lax.Precision.HIGHEST, preferred_element_type=F32)
        decay_last = decay_cols[c_rows - 1:c_rows, :]
        cd_ref[j] = jnp.broadcast_to(jnp.exp(decay_last), (SUBLANES, LANES))
        for h in heads:
            cols = head_cols[h]
            qn = qkv_ref[3 * slot(j, h)] * (GDN_HEAD_DIM ** -0.5)
            kn = qkv_ref[3 * slot(j, h) + 1]
            v = qkv_ref[3 * slot(j, h) + 2]
            dcol = decay_cols[:, h:h + 1]
            drow = decay_rows[h:h + 1, :]
            lmask = jnp.where(causal, jnp.exp(jnp.where(causal, dcol - drow, 0.0)), 0.0)
            e_dec = jnp.exp(dcol)
            e_rem = jnp.exp(decay_last[:, h:h + 1] - dcol)
            beta = beta_all[:, GDN_HEADS + h:GDN_HEADS + h + 1]
            kbeta = kn * beta
            kn_b = kn.astype(BF16)
            a_ref[slot(j, h)] = jnp.where(strict, _dot_nt(kbeta.astype(BF16), kn_b) * lmask, 0.0)
            intra_ref[j, :, cols] = jnp.where(causal, _dot_nt(qn.astype(BF16), kn_b) * lmask, 0.0).astype(BF16)
            rhs_ref[slot(j, h)] = jnp.concatenate([v * beta, kbeta * e_dec], axis=1).astype(BF16)
            qd_ref[j, :, cols] = (qn * e_dec).astype(BF16)
            kdt_ref[j, :, cols] = (kn * e_rem).T.astype(BF16)


def _gdn(gqkv, gab, gz, conv_w, gate_params, norm_g, batch, seq):
    t = gqkv.shape[0]
    group = GDN_STEP_CHUNKS * GDN_CHUNK
    groups_per_seq = seq // group
    n_groups = batch * groups_per_seq
    per8 = group // SUBLANES
    n_slots = GDN_STEP_CHUNKS * GDN_HEADS
    cur = lambda s: jnp.minimum(s, n_groups - 1)
    done = lambda s: jnp.maximum(s - 1, 0)
    vmem = lambda shape, dt: pltpu.VMEM(shape, dt)
    return pl.pallas_call(
        functools.partial(_gdn_body, groups_per_seq=groups_per_seq, n_groups=n_groups),
        grid=(n_groups + 1,),
        in_specs=[_resident((SUBLANES, LANES)), _resident((CONV_K, 3 * GDN_W)),
                  _resident((1, GDN_HEAD_DIM)),
                  pl.BlockSpec((group, 3 * GDN_W), lambda s: (cur(s), 0)),
                  pl.BlockSpec((SUBLANES, 3 * GDN_W), lambda s: (jnp.maximum(cur(s) * per8 - 1, 0), 0)),
                  pl.BlockSpec((group, LANES), lambda s: (cur(s), 0)),
                  pl.BlockSpec((group, GDN_W), lambda s: (done(s), 0))],
        out_specs=pl.BlockSpec((group, GDN_W), lambda s: (done(s), 0)),
        out_shape=jax.ShapeDtypeStruct((t, GDN_W), BF16),
        scratch_shapes=[vmem((SUBLANES + group, 3 * GDN_W), F32),
                        vmem((n_slots, GDN_CHUNK, GDN_CHUNK), F32),
                        vmem((n_slots, GDN_CHUNK, 2 * GDN_HEAD_DIM), BF16),
                        vmem((GDN_STEP_CHUNKS, GDN_CHUNK, GDN_W), BF16),
                        vmem((GDN_STEP_CHUNKS, GDN_HEAD_DIM, GDN_HEADS * GDN_CHUNK), BF16),
                        vmem((GDN_STEP_CHUNKS, GDN_CHUNK, GDN_HEADS * GDN_CHUNK), BF16),
                        vmem((GDN_STEP_CHUNKS, SUBLANES, LANES), F32),
                        vmem((GDN_HEADS, GDN_HEAD_DIM, GDN_HEAD_DIM), F32),
                        vmem((3 * n_slots, GDN_CHUNK, GDN_HEAD_DIM), F32)],
        compiler_params=_cparams("arbitrary"),
        name="gdn",
    )(gate_params, conv_w, norm_g, gqkv, gqkv, gab, gz)


def _memkv_body(m_ref, w_ref, k_ref, v_ref):
    mb = m_ref[...].astype(BF16)
    k_ref[...] = _dot(mb, w_ref[:, 0:D_MODEL]).astype(BF16)
    v_ref[...] = _dot(mb, w_ref[:, D_MODEL:2 * D_MODEL]).astype(BF16)


def _memkv(mem, w_kv):
    rows = mem.shape[0]
    tile = min(ROW_TILE, rows)
    row = pl.BlockSpec((tile, D_MODEL), lambda i: (i, 0))
    return pl.pallas_call(
        _memkv_body,
        grid=(rows // tile,),
        in_specs=[row, _resident((D_MODEL, 2 * D_MODEL))],
        out_specs=[row, row],
        out_shape=[jax.ShapeDtypeStruct((rows, D_MODEL), BF16)] * 2,
        compiler_params=_cparams("parallel"),
        name="mem_kv_proj",
    )(mem, w_kv)


def _mix_xattn_body(att_ref, gdn_ref, x_ref, wmix_ref, wq_ref, k_ref, v_ref, wo_ref,
                    g0_ref, b0_ref, g1_ref, b1_ref, o_ref):
    n_sub = ROW_TILE // SUB_ROWS
    rows = [slice(r * SUB_ROWS, (r + 1) * SUB_ROWS) for r in range(n_sub)]
    head_cols = [slice(h * X_HEAD_DIM, (h + 1) * X_HEAD_DIM) for h in range(X_HEADS)]

    def mix(r):
        y = (_dot(att_ref[rows[r], :], wmix_ref[0:ATT_Q, :])
             + _dot(gdn_ref[rows[r], :], wmix_ref[ATT_Q:ATT_Q + GDN_W, :]))
        return _layer_norm(ALPHA * x_ref[rows[r], :] + y, g0_ref[...], b0_ref[...])

    def project_q(x1):
        return (_dot(x1.astype(BF16), wq_ref[...]) * X_SCALE).astype(BF16)

    def scores(q):
        return [_dot_nt(q[:, c], k_ref[:, c]) for c in head_cols]

    def softmax(s_heads):
        out = []
        for s in s_heads:
            p = jnp.exp(s - jnp.max(s, axis=-1, keepdims=True))
            out.append((p * (1.0 / jnp.sum(p, axis=-1, keepdims=True))).astype(BF16))
        return out

    def attend(p_heads):
        return jnp.concatenate([_dot(p, v_ref[:, c]).astype(BF16) for p, c in zip(p_heads, head_cols)], axis=1)

    def finish(r, x1, heads):
        y = _dot(heads, wo_ref[...])
        o_ref[rows[r], :] = _layer_norm(ALPHA * x1 + y, g1_ref[...], b1_ref[...])

    x1 = [mix(r) for r in range(n_sub)]
    s_prev = scores(project_q(x1[0]))
    for r in range(n_sub):
        q_next = project_q(x1[r + 1]) if r + 1 < n_sub else None
        heads = attend(softmax(s_prev))
        if q_next is not None:
            s_prev = scores(q_next)
        finish(r, x1[r], heads)


def _mix_xattn(att, gdn, x, w_mix, wq, k_mem, v_mem, wo, ln0, ln1, batch, seq):
    t = x.shape[0]
    steps = seq // ROW_TILE
    mem_len = k_mem.shape[0] // batch
    row = lambda n: pl.BlockSpec((ROW_TILE, n), lambda bb, i: (bb * steps + i, 0))
    mem = pl.BlockSpec((mem_len, D_MODEL), lambda bb, i: (bb, 0))
    weight = _resident((D_MODEL, D_MODEL))
    vec = _resident((1, D_MODEL))
    return pl.pallas_call(
        _mix_xattn_body,
        grid=(batch, steps),
        in_specs=[row(ATT_Q), row(GDN_W), row(D_MODEL), weight, weight, mem, mem, weight, vec, vec, vec, vec],
        out_specs=row(D_MODEL),
        out_shape=jax.ShapeDtypeStruct((t, D_MODEL), F32),
        compiler_params=_cparams("parallel", "parallel"),
        name="mix_xattn_ln",
    )(att, gdn, x, w_mix, wq, k_mem, v_mem, wo, *ln0, *ln1)


def _mlp_ln_body(x_ref, w1_ref, w2_ref, g_ref, b_ref, o_ref):
    x = x_ref[...]
    xb = x.astype(BF16)
    y = None
    for c in range(D_FF // FF_CHUNK):
        cols = slice(c * FF_CHUNK, (c + 1) * FF_CHUNK)
        hid = jnp.square(jnp.maximum(_dot(xb, w1_ref[:, cols]), 0.0)).astype(BF16)
        part = _dot(hid, w2_ref[cols, :])
        y = part if y is None else y + part
    o_ref[...] = _layer_norm(ALPHA * x + y, g_ref[...], b_ref[...])


def _mlp_ln(x, w1, w2, g, b):
    t = x.shape[0]
    row = pl.BlockSpec((ROW_TILE, D_MODEL), lambda i: (i, 0))
    return pl.pallas_call(
        _mlp_ln_body,
        grid=(t // ROW_TILE,),
        in_specs=[row, _resident((D_MODEL, D_FF)), _resident((D_FF, D_MODEL)),
                  _resident((1, D_MODEL)), _resident((1, D_MODEL))],
        out_specs=row,
        out_shape=jax.ShapeDtypeStruct((t, D_MODEL), F32),
        compiler_params=_cparams("parallel"),
        name="mlp_ln",
    )(x, w1, w2, g, b)


def _pack_w_in(w_in):
    o = 0
    aq = w_in[:, o:o + ATT_Q]; o += ATT_Q
    ak = w_in[:, o:o + ATT_KV]; o += ATT_KV
    av = w_in[:, o:o + ATT_KV]; o += ATT_KV
    gqkv = w_in[:, o:o + 3 * GDN_W]; o += 3 * GDN_W
    ga = w_in[:, o:o + GDN_HEADS]; o += GDN_HEADS
    gb = w_in[:, o:o + GDN_HEADS]; o += GDN_HEADS
    gz = w_in[:, o:o + GDN_W]
    dup = lambda m: jnp.concatenate(
        [m[:, h * ATT_HEAD_DIM:(h + 1) * ATT_HEAD_DIM] for h in range(ATT_KV_HEADS) for _ in range(2)], axis=1)
    pad = jnp.zeros((w_in.shape[0], LANES - 2 * GDN_HEADS), w_in.dtype)
    return jnp.concatenate([aq, dup(ak), dup(av), gqkv, gz, ga, gb, pad], axis=1).astype(BF16)


def _pad_lanes(v):
    return jnp.pad(v, (0, LANES - v.shape[0]))


def kernel(x, mem, w_in, conv_w, attn_sinks, a_log, dt_bias, gdn_norm_g, w_mix_out,
           wq_mem, wk_mem, wv_mem, wo_mem, w_ff1, w_ff2, ln_g, ln_b):
    batch, seq, _ = x.shape
    xt = x.reshape(batch * seq, D_MODEL)
    memt = mem.reshape(batch * mem.shape[1], D_MODEL)
    for l in range(DEPTH):
        w_all = _pack_w_in(w_in[l])
        gate_params = jnp.zeros((SUBLANES, LANES), F32)
        gate_params = gate_params.at[0].set(_pad_lanes(a_log[l])).at[1].set(_pad_lanes(dt_bias[l]))
        ln = lambda i: (ln_g[l, i].reshape(1, D_MODEL), ln_b[l, i].reshape(1, D_MODEL))

        aq, akv, gqkv, gz, gab = _inproj(xt, w_all)
        att = _swa(aq, akv, attn_sinks[l], batch, seq)
        gdn = _gdn(gqkv, gab, gz, conv_w[l], gate_params, gdn_norm_g[l].reshape(1, GDN_HEAD_DIM), batch, seq)
        w_kv = jnp.concatenate([wk_mem[l], wv_mem[l]], axis=1).astype(BF16)
        k_mem, v_mem = _memkv(memt, w_kv)
        xt = _mix_xattn(att, gdn, xt, w_mix_out[l].astype(BF16), wq_mem[l].astype(BF16), k_mem, v_mem,
                        wo_mem[l].astype(BF16), ln(0), ln(1), batch, seq)

        xt = _mlp_ln(xt, w_ff1[l].astype(BF16), w_ff2[l].astype(BF16), *ln(2))
    return xt.reshape(batch, seq, D_MODEL)
```

```python
import functools

import jax
import jax.numpy as jnp
from jax import lax
from jax.experimental import pallas as pl
from jax.experimental.pallas import tpu as pltpu

F32 = jnp.float32
BF16 = jnp.bfloat16

D_MODEL = 1024
DEPTH = 2
ATT_HEADS = 8
ATT_KV_HEADS = 2
ATT_HEAD_DIM = 64
ATT_GROUP = ATT_HEADS // ATT_KV_HEADS
WINDOW = 128
ATT_Q = ATT_HEADS * ATT_HEAD_DIM
ATT_KV = ATT_KV_HEADS * ATT_HEAD_DIM
GDN_HEADS = 4
GDN_HEAD_DIM = 128
GDN_W = GDN_HEADS * GDN_HEAD_DIM
CONV_K = 4
X_HEADS = 4
X_HEAD_DIM = D_MODEL // X_HEADS
D_FF = 4 * D_MODEL
ALPHA = (2 * DEPTH) ** 0.25
LN_EPS = 1e-5
RMS_EPS = 1e-6
NEG_INF = -1e30
ATT_SCALE = ATT_HEAD_DIM ** -0.5
X_SCALE = X_HEAD_DIM ** -0.5
ALIBI_SLOPES = tuple(2.0 ** (-8.0 * (h + 1) / ATT_HEADS) for h in range(ATT_HEADS))

LANES = 128
SUBLANES = 8
ROW_TILE = 512
SUB_ROWS = 256
ATT_ROWS = 512
GDN_CHUNK = 128
GDN_STEP_CHUNKS = 4
FF_CHUNK = 512
VMEM_LIMIT = 56 * 1024 * 1024

_C_AQ = (0, ATT_Q)
_C_AKV = (_C_AQ[1], _C_AQ[1] + 4 * LANES)
_C_GQKV = (_C_AKV[1], _C_AKV[1] + 3 * GDN_W)
_C_GZ = (_C_GQKV[1], _C_GQKV[1] + GDN_W)
_C_GAB = (_C_GZ[1], _C_GZ[1] + LANES)
N_IN = _C_GAB[1]


def _cparams(*sem):
    return pltpu.CompilerParams(dimension_semantics=sem, vmem_limit_bytes=VMEM_LIMIT)


def _resident(shape):
    nd = len(shape)
    return pl.BlockSpec(shape, lambda *_: (0,) * nd, pipeline_mode=pl.Buffered(1))


def _layer_weight(layer, shape):
    nd = len(shape)
    return pl.BlockSpec((None,) + tuple(shape), lambda *_: (layer,) + (0,) * nd, pipeline_mode=pl.Buffered(1))


def _dot(a, b):
    return jnp.dot(a, b, preferred_element_type=F32)


def _dot_nt(a, b):
    return lax.dot_general(a, b, (((1,), (1,)), ((), ())), preferred_element_type=F32)


def _layer_norm(xf, g, b):
    mu = jnp.mean(xf, axis=-1, keepdims=True)
    d = xf - mu
    var = jnp.mean(d * d, axis=-1, keepdims=True)
    return d * lax.rsqrt(var + LN_EPS) * g + b


def _silu(x):
    return x * jax.nn.sigmoid(x)


def _inproj_body(x_ref, w_ref, aq_ref, akv_ref, gqkv_ref, gz_ref, gab_ref):
    xb = x_ref[...].astype(BF16)

    def mm(lo, hi):
        return _dot(xb, w_ref[:, lo:hi])

    aq_ref[...] = mm(*_C_AQ).astype(BF16)
    akv_ref[...] = mm(*_C_AKV).astype(BF16)
    for c in range(3):
        lo = _C_GQKV[0] + c * GDN_W
        gqkv_ref[:, c * GDN_W:(c + 1) * GDN_W] = mm(lo, lo + GDN_W)
    gz_ref[...] = mm(*_C_GZ)
    gab_ref[...] = mm(*_C_GAB)


def _inproj(x, w_all, layer):
    t = x.shape[0]
    row = lambda n: pl.BlockSpec((ROW_TILE, n), lambda i: (i, 0))
    return pl.pallas_call(
        _inproj_body,
        grid=(t // ROW_TILE,),
        in_specs=[row(D_MODEL), _layer_weight(layer, (D_MODEL, N_IN))],
        out_specs=[row(ATT_Q), row(4 * LANES), row(3 * GDN_W), row(GDN_W), row(LANES)],
        out_shape=[jax.ShapeDtypeStruct((t, ATT_Q), BF16),
                   jax.ShapeDtypeStruct((t, 4 * LANES), BF16),
                   jax.ShapeDtypeStruct((t, 3 * GDN_W), F32),
                   jax.ShapeDtypeStruct((t, GDN_W), F32),
                   jax.ShapeDtypeStruct((t, LANES), F32)],
        compiler_params=_cparams("parallel"),
        name="in_proj",
    )(x, w_all)


def _swa_body(sinks_ref, q_ref, kvc_ref, kvp_ref, o_ref):
    first = pl.program_id(1) == 0
    lane = lax.broadcasted_iota(jnp.int32, (1, LANES), 1)
    q_masks = (jnp.where(lane < ATT_HEAD_DIM, ATT_SCALE, 0.0).astype(BF16),
               jnp.where(lane >= ATT_HEAD_DIM, ATT_SCALE, 0.0).astype(BF16))
    low_half = lax.broadcasted_iota(jnp.int32, (WINDOW, LANES), 1) < ATT_HEAD_DIM
    qi = lax.broadcasted_iota(jnp.int32, (WINDOW, WINDOW), 0)
    kj = lax.broadcasted_iota(jnp.int32, (WINDOW, WINDOW), 1)
    in_cur = kj <= qi
    distf = jnp.where(in_cur, qi - kj, qi + WINDOW - kj).astype(F32)
    visible_first = in_cur | (kj >= jnp.where(first, WINDOW, 0))

    n_blk = ATT_ROWS // WINDOW
    rows = [slice(blk * WINDOW, (blk + 1) * WINDOW) for blk in range(n_blk)]

    def band(blk):
        prev = kvp_ref[...] if blk == 0 else kvc_ref[rows[blk - 1], :]
        return jnp.concatenate([prev, kvc_ref[rows[blk], :]], axis=0)

    def scores(blk):
        kv = band(blk)
        out = []
        for hq in range(ATT_HEADS):
            kk = kv[:, (hq // ATT_GROUP) * LANES:(hq // ATT_GROUP + 1) * LANES]
            qp = q_ref[rows[blk], (hq // 2) * LANES:(hq // 2 + 1) * LANES]
            out.append(_dot_nt(qp * q_masks[hq % 2], kk))
        return out

    def attend(blk, s2_heads):
        kv = band(blk)
        halves = []
        for hq, s2 in enumerate(s2_heads):
            vv = kv[:, (2 + hq // ATT_GROUP) * LANES:(3 + hq // ATT_GROUP) * LANES]
            s = jnp.where(in_cur, s2[:, WINDOW:], s2[:, :WINDOW]) - ALIBI_SLOPES[hq] * distf
            if blk == 0:
                s = jnp.where(visible_first, s, NEG_INF)
            sink = sinks_ref[hq]
            m = jnp.maximum(jnp.max(s, axis=-1, keepdims=True), sink)
            p = jnp.exp(s - m)
            denom = jnp.sum(p, axis=-1, keepdims=True) + jnp.exp(sink - m)
            pn = p * (1.0 / denom)
            p2 = jnp.concatenate([jnp.where(in_cur, 0.0, pn), jnp.where(in_cur, pn, 0.0)], axis=1)
            halves.append(_dot(p2.astype(BF16), vv))
        for pair in range(ATT_HEADS // 2):
            o_ref[rows[blk], pair * LANES:(pair + 1) * LANES] = jnp.where(
                low_half, halves[2 * pair], halves[2 * pair + 1]).astype(BF16)

    s2_heads = scores(0)
    for blk in range(n_blk):
        s2_next = scores(blk + 1) if blk + 1 < n_blk else None
        attend(blk, s2_heads)
        s2_heads = s2_next


def _swa(aq, akv, sinks, batch, seq):
    t = aq.shape[0]
    steps = seq // ATT_ROWS
    blocks_per_step = ATT_ROWS // WINDOW
    blocks_per_seq = seq // WINDOW

    def prev_map(b, j):
        return (b * blocks_per_seq + jnp.maximum(j * blocks_per_step - 1, 0), 0)

    return pl.pallas_call(
        _swa_body,
        grid=(batch, steps),
        in_specs=[pl.BlockSpec(memory_space=pltpu.SMEM),
                  pl.BlockSpec((ATT_ROWS, ATT_Q), lambda b, j: (b * steps + j, 0)),
                  pl.BlockSpec((ATT_ROWS, 4 * LANES), lambda b, j: (b * steps + j, 0)),
                  pl.BlockSpec((WINDOW, 4 * LANES), prev_map)],
        out_specs=pl.BlockSpec((ATT_ROWS, ATT_Q), lambda b, j: (b * steps + j, 0)),
        out_shape=jax.ShapeDtypeStruct((t, ATT_Q), BF16),
        compiler_params=_cparams("parallel", "parallel"),
        name="swa_attention",
    )(sinks, aq, akv, akv)


def _unit_lower_inverse_stages(mats, ii, jj, out):
    base = 16

    def same_block(size):
        return (ii // size) == (jj // size)

    def bf(ms):
        return [m.astype(BF16) for m in ms]

    eye = (ii == jj).astype(F32)
    in_base = same_block(base)
    b1 = [jnp.where(in_base, -a, 0.0) for a in mats]
    b1_b = bf(b1)
    b2_b = bf([_dot(b, b) for b in b1_b])
    yield
    t = [eye + b for b in b1]
    b4 = [_dot(b, b) for b in b2_b]
    t = [x + _dot(xb, p) for x, xb, p in zip(t, bf(t), b2_b)]
    b4_b = bf(b4)
    yield
    b8 = [_dot(b, b) for b in b4_b]
    t = [x + _dot(xb, p) for x, xb, p in zip(t, bf(t), b4_b)]
    b8_b = bf(b8)
    yield
    t = [x + _dot(xb, p) for x, xb, p in zip(t, bf(t), b8_b)]
    yield
    size = base
    while size < GDN_CHUNK:
        off_diag = same_block(2 * size) & jnp.logical_not(same_block(size))
        t_b = bf(t)
        ct_b = bf([_dot(jnp.where(off_diag, a, 0.0).astype(BF16), xb) for a, xb in zip(mats, t_b)])
        yield
        t = [x - _dot(xb, y) for x, xb, y in zip(t, t_b, ct_b)]
        yield
        size *= 2
    out.extend(t)


def _gdn_body(gp_ref, convw_ref, normg_ref, x_ref, xprev_ref, gab_ref, z_ref, o_ref,
              xs_ref, a_ref, rhs_ref, qd_ref, kdt_ref, intra_ref, cd_ref, state_ref, qkv_ref,
              *, groups_per_seq, n_groups):
    step = pl.program_id(0)
    c_rows = GDN_CHUNK
    subs = range(GDN_STEP_CHUNKS)
    heads = range(GDN_HEADS)
    pairs = [(j, h) for j in subs for h in heads]
    slot = lambda j, h: j * GDN_HEADS + h
    head_cols = [slice(h * GDN_HEAD_DIM, (h + 1) * GDN_HEAD_DIM) for h in heads]
    chunk_rows = [slice(j * c_rows, (j + 1) * c_rows) for j in subs]

    @pl.when(step == 0)
    def _():
        for ref in (a_ref, rhs_ref, qd_ref, kdt_ref, intra_ref, cd_ref, state_ref):
            ref[...] = jnp.zeros_like(ref)

    ii = lax.broadcasted_iota(jnp.int32, (c_rows, c_rows), 0)
    jj = lax.broadcasted_iota(jnp.int32, (c_rows, c_rows), 1)
    causal = jj <= ii
    strict = jj < ii

    def finish():
        t_inv = []
        yield from _unit_lower_inverse_stages([a_ref[slot(j, h)] for j, h in pairs], ii, jj, t_inv)
        uw = [_dot(t.astype(BF16), rhs_ref[slot(j, h)]) for (j, h), t in zip(pairs, t_inv)]
        yield
        keep = jnp.where(lax.rem(step - 1, groups_per_seq) == 0, 0.0, 1.0)
        state = [state_ref[h] * keep for h in heads]
        for j in subs:
            r = [_dot(jnp.concatenate([uw[slot(j, h)][:, GDN_HEAD_DIM:].astype(BF16),
                                       qd_ref[j, :, head_cols[h]]], axis=0),
                      state[h].astype(BF16)) for h in heads]
            yield
            v_new_b = [(uw[slot(j, h)][:, :GDN_HEAD_DIM] - r[h][:c_rows]).astype(BF16) for h in heads]
            state = [state[h] * cd_ref[j, 0:1, h:h + 1] + _dot(kdt_ref[j, :, head_cols[h]], v_new_b[h])
                     for h in heads]
            yield
            for h in heads:
                o = r[h][c_rows:] + _dot(intra_ref[j, :, head_cols[h]], v_new_b[h])
                o = (o * lax.rsqrt(jnp.mean(o * o, axis=-1, keepdims=True) + RMS_EPS)
                     * normg_ref[...] * _silu(z_ref[chunk_rows[j], head_cols[h]]))
                o_ref[chunk_rows[j], head_cols[h]] = o.astype(BF16)
            yield
        for h in heads:
            state_ref[h] = state[h]

    first = lax.rem(jnp.minimum(step, n_groups - 1), groups_per_seq) == 0
    xs_ref[0:SUBLANES, :] = jnp.where(first, 0.0, xprev_ref[...])
    xs_ref[SUBLANES:SUBLANES + GDN_STEP_CHUNKS * c_rows, :] = x_ref[...]

    def convolve():
        for j, h in pairs:
            for part in range(3):
                col = part * GDN_W + h * GDN_HEAD_DIM
                acc = None
                for tap in range(CONV_K):
                    lo = SUBLANES + j * c_rows - (CONV_K - 1) + tap
                    term = convw_ref[tap:tap + 1, col:col + LANES] * xs_ref[lo:lo + c_rows, col:col + LANES]
                    acc = term if acc is None else acc + term
                y = _silu(acc)
                if part < 2:
                    y = y * lax.rsqrt(jnp.sum(y * y, axis=-1, keepdims=True) + RMS_EPS)
                qkv_ref[3 * slot(j, h) + part] = y
                yield

    finishing, convolving = finish(), convolve()
    n_finish, n_conv = 11 + 3 * GDN_STEP_CHUNKS, 3 * len(pairs)
    done = 0
    for turn, _ in enumerate(finishing):
        while done * n_finish < (turn + 1) * n_conv and next(convolving, True) is None:
            done += 1
    for _ in convolving:
        pass

    a_log = gp_ref[0:1, :]
    dt_bias = gp_ref[1:2, :]
    ltri = causal.astype(F32)
    for j in subs:
        gab = gab_ref[chunk_rows[j], :]
        g_all = -jnp.exp(a_log) * jax.nn.softplus(gab + dt_bias)
        beta_all = jax.nn.sigmoid(gab)
        decay_cols = jnp.dot(ltri, g_all, precision=lax.Precision.HIGHEST, preferred_element_type=F32)
        decay_rows = lax.dot_general(g_all.T[0:SUBLANES, :], ltri, (((1,), (1,)), ((), ())),
                                     precision=lax.Precision.HIGHEST, preferred_element_type=F32)
        decay_last = decay_cols[c_rows - 1:c_rows, :]
        cd_ref[j] = jnp.broadcast_to(jnp.exp(decay_last), (SUBLANES, LANES))
        for h in heads:
            cols = head_cols[h]
            qn = qkv_ref[3 * slot(j, h)] * (GDN_HEAD_DIM ** -0.5)
            kn = qkv_ref[3 * slot(j, h) + 1]
            v = qkv_ref[3 * slot(j, h) + 2]
            dcol = decay_cols[:, h:h + 1]
            drow = decay_rows[h:h + 1, :]
            lmask = jnp.where(causal, jnp.exp(jnp.where(causal, dcol - drow, 0.0)), 0.0)
            e_dec = jnp.exp(dcol)
            e_rem = jnp.exp(decay_last[:, h:h + 1] - dcol)
            beta = beta_all[:, GDN_HEADS + h:GDN_HEADS + h + 1]
            kbeta = kn * beta
            kn_b = kn.astype(BF16)
            a_ref[slot(j, h)] = jnp.where(strict, _dot_nt(kbeta.astype(BF16), kn_b) * lmask, 0.0)
            intra_ref[j, :, cols] = jnp.where(causal, _dot_nt(qn.astype(BF16), kn_b) * lmask, 0.0).astype(BF16)
            rhs_ref[slot(j, h)] = jnp.concatenate([v * beta, kbeta * e_dec], axis=1).astype(BF16)
            qd_ref[j, :, cols] = (qn * e_dec).astype(BF16)
            kdt_ref[j, :, cols] = (kn * e_rem).T.astype(BF16)


def _gdn(gqkv, gab, gz, conv_w, gate_params, norm_g, batch, seq):
    t = gqkv.shape[0]
    group = GDN_STEP_CHUNKS * GDN_CHUNK
    groups_per_seq = seq // group
    n_groups = batch * groups_per_seq
    per8 = group // SUBLANES
    n_slots = GDN_STEP_CHUNKS * GDN_HEADS
    cur = lambda s: jnp.minimum(s, n_groups - 1)
    done = lambda s: jnp.maximum(s - 1, 0)
    vmem = lambda shape, dt: pltpu.VMEM(shape, dt)
    return pl.pallas_call(
        functools.partial(_gdn_body, groups_per_seq=groups_per_seq, n_groups=n_groups),
        grid=(n_groups + 1,),
        in_specs=[_resident((SUBLANES, LANES)), _resident((CONV_K, 3 * GDN_W)),
                  _resident((1, GDN_HEAD_DIM)),
                  pl.BlockSpec((group, 3 * GDN_W), lambda s: (cur(s), 0)),
                  pl.BlockSpec((SUBLANES, 3 * GDN_W), lambda s: (jnp.maximum(cur(s) * per8 - 1, 0), 0)),
                  pl.BlockSpec((group, LANES), lambda s: (cur(s), 0)),
                  pl.BlockSpec((group, GDN_W), lambda s: (done(s), 0))],
        out_specs=pl.BlockSpec((group, GDN_W), lambda s: (done(s), 0)),
        out_shape=jax.ShapeDtypeStruct((t, GDN_W), BF16),
        scratch_shapes=[vmem((SUBLANES + group, 3 * GDN_W), F32),
                        vmem((n_slots, GDN_CHUNK, GDN_CHUNK), F32),
                        vmem((n_slots, GDN_CHUNK, 2 * GDN_HEAD_DIM), BF16),
                        vmem((GDN_STEP_CHUNKS, GDN_CHUNK, GDN_W), BF16),
                        vmem((GDN_STEP_CHUNKS, GDN_HEAD_DIM, GDN_HEADS * GDN_CHUNK), BF16),
                        vmem((GDN_STEP_CHUNKS, GDN_CHUNK, GDN_HEADS * GDN_CHUNK), BF16),
                        vmem((GDN_STEP_CHUNKS, SUBLANES, LANES), F32),
                        vmem((GDN_HEADS, GDN_HEAD_DIM, GDN_HEAD_DIM), F32),
                        vmem((3 * n_slots, GDN_CHUNK, GDN_HEAD_DIM), F32)],
        compiler_params=_cparams("arbitrary"),
        name="gdn",
    )(gate_params, conv_w, norm_g, gqkv, gqkv, gab, gz)


def _memkv_body(m_ref, w_ref, k_ref, v_ref):
    mb = m_ref[...].astype(BF16)
    k_ref[...] = _dot(mb, w_ref[:, 0:D_MODEL]).astype(BF16)
    v_ref[...] = _dot(mb, w_ref[:, D_MODEL:2 * D_MODEL]).astype(BF16)


def _memkv(mem, w_kv, layer):
    rows = mem.shape[0]
    tile = min(ROW_TILE, rows)
    row = pl.BlockSpec((tile, D_MODEL), lambda i: (i, 0))
    return pl.pallas_call(
        _memkv_body,
        grid=(rows // tile,),
        in_specs=[row, _layer_weight(layer, (D_MODEL, 2 * D_MODEL))],
        out_specs=[row, row],
        out_shape=[jax.ShapeDtypeStruct((rows, D_MODEL), BF16)] * 2,
        compiler_params=_cparams("parallel"),
        name="mem_kv_proj",
    )(mem, w_kv)


def _mix_xattn_body(att_ref, gdn_ref, x_ref, wmix_ref, wq_ref, k_ref, v_ref, wo_ref,
                    g0_ref, b0_ref, g1_ref, b1_ref, o_ref):
    n_sub = ROW_TILE // SUB_ROWS
    rows = [slice(r * SUB_ROWS, (r + 1) * SUB_ROWS) for r in range(n_sub)]
    head_cols = [slice(h * X_HEAD_DIM, (h + 1) * X_HEAD_DIM) for h in range(X_HEADS)]

    def mix(r):
        y = (_dot(att_ref[rows[r], :], wmix_ref[0:ATT_Q, :])
             + _dot(gdn_ref[rows[r], :], wmix_ref[ATT_Q:ATT_Q + GDN_W, :]))
        return _layer_norm(ALPHA * x_ref[rows[r], :] + y, g0_ref[...], b0_ref[...])

    def project_q(x1):
        return (_dot(x1.astype(BF16), wq_ref[...]) * X_SCALE).astype(BF16)

    def scores(q):
        return [_dot_nt(q[:, c], k_ref[:, c]) for c in head_cols]

    def softmax(s_heads):
        out = []
        for s in s_heads:
            p = jnp.exp(s - jnp.max(s, axis=-1, keepdims=True))
            out.append((p * (1.0 / jnp.sum(p, axis=-1, keepdims=True))).astype(BF16))
        return out

    def attend(p_heads):
        return jnp.concatenate([_dot(p, v_ref[:, c]).astype(BF16) for p, c in zip(p_heads, head_cols)], axis=1)

    def finish(r, x1, heads):
        y = _dot(heads, wo_ref[...])
        o_ref[rows[r], :] = _layer_norm(ALPHA * x1 + y, g1_ref[...], b1_ref[...])

    x1 = [mix(r) for r in range(n_sub)]
    s_prev = scores(project_q(x1[0]))
    for r in range(n_sub):
        q_next = project_q(x1[r + 1]) if r + 1 < n_sub else None
        heads = attend(softmax(s_prev))
        if q_next is not None:
            s_prev = scores(q_next)
        finish(r, x1[r], heads)


def _mix_xattn(att, gdn, x, w_mix, wq, k_mem, v_mem, wo, ln0, ln1, layer, batch, seq):
    t = x.shape[0]
    steps = seq // ROW_TILE
    mem_len = k_mem.shape[0] // batch
    row = lambda n: pl.BlockSpec((ROW_TILE, n), lambda bb, i: (bb * steps + i, 0))
    mem = pl.BlockSpec((mem_len, D_MODEL), lambda bb, i: (bb, 0))
    weight = _layer_weight(layer, (D_MODEL, D_MODEL))
    vec = _resident((1, D_MODEL))
    return pl.pallas_call(
        _mix_xattn_body,
        grid=(batch, steps),
        in_specs=[row(ATT_Q), row(GDN_W), row(D_MODEL), weight, weight, mem, mem, weight, vec, vec, vec, vec],
        out_specs=row(D_MODEL),
        out_shape=jax.ShapeDtypeStruct((t, D_MODEL), F32),
        compiler_params=_cparams("parallel", "parallel"),
        name="mix_xattn_ln",
    )(att, gdn, x, w_mix, wq, k_mem, v_mem, wo, *ln0, *ln1)


def _mlp_ln_body(x_ref, w1_ref, w2_ref, g_ref, b_ref, o_ref):
    x = x_ref[...]
    xb = x.astype(BF16)
    y = None
    for c in range(D_FF // FF_CHUNK):
        cols = slice(c * FF_CHUNK, (c + 1) * FF_CHUNK)
        hid = jnp.square(jnp.maximum(_dot(xb, w1_ref[:, cols]), 0.0)).astype(BF16)
        part = _dot(hid, w2_ref[cols, :])
        y = part if y is None else y + part
    o_ref[...] = _layer_norm(ALPHA * x + y, g_ref[...], b_ref[...])


def _mlp_ln(x, w1, w2, g, b, layer):
    t = x.shape[0]
    row = pl.BlockSpec((ROW_TILE, D_MODEL), lambda i: (i, 0))
    return pl.pallas_call(
        _mlp_ln_body,
        grid=(t // ROW_TILE,),
        in_specs=[row, _layer_weight(layer, (D_MODEL, D_FF)), _layer_weight(layer, (D_FF, D_MODEL)),
                  _resident((1, D_MODEL)), _resident((1, D_MODEL))],
        out_specs=row,
        out_shape=jax.ShapeDtypeStruct((t, D_MODEL), F32),
        compiler_params=_cparams("parallel"),
        name="mlp_ln",
    )(x, w1, w2, g, b)


def _pack_w_in(w_in):
    w = w_in.astype(BF16)
    o = 0
    aq = w[..., o:o + ATT_Q]; o += ATT_Q
    ak = w[..., o:o + ATT_KV]; o += ATT_KV
    av = w[..., o:o + ATT_KV]; o += ATT_KV
    gqkv = w[..., o:o + 3 * GDN_W]; o += 3 * GDN_W
    ga = w[..., o:o + GDN_HEADS]; o += GDN_HEADS
    gb = w[..., o:o + GDN_HEADS]; o += GDN_HEADS
    gz = w[..., o:o + GDN_W]
    dup = lambda m: jnp.concatenate(
        [m[..., h * ATT_HEAD_DIM:(h + 1) * ATT_HEAD_DIM] for h in range(ATT_KV_HEADS) for _ in range(2)], axis=-1)
    pad = jnp.zeros(w.shape[:-1] + (LANES - 2 * GDN_HEADS,), BF16)
    return jnp.concatenate([aq, dup(ak), dup(av), gqkv, gz, ga, gb, pad], axis=-1)


def _pad_lanes(v):
    return jnp.pad(v, (0, LANES - v.shape[0]))


def kernel(x, mem, w_in, conv_w, attn_sinks, a_log, dt_bias, gdn_norm_g, w_mix_out,
           wq_mem, wk_mem, wv_mem, wo_mem, w_ff1, w_ff2, ln_g, ln_b):
    batch, seq, _ = x.shape
    xt = x.reshape(batch * seq, D_MODEL)
    memt = mem.reshape(batch * mem.shape[1], D_MODEL)
    w_all = _pack_w_in(w_in)
    w_mix, wq, wo = w_mix_out.astype(BF16), wq_mem.astype(BF16), wo_mem.astype(BF16)
    w_kv = jnp.concatenate([wk_mem.astype(BF16), wv_mem.astype(BF16)], axis=-1)
    w1, w2 = w_ff1.astype(BF16), w_ff2.astype(BF16)
    for l in range(DEPTH):
        gate_params = jnp.zeros((SUBLANES, LANES), F32)
        gate_params = gate_params.at[0].set(_pad_lanes(a_log[l])).at[1].set(_pad_lanes(dt_bias[l]))
        ln = lambda i: (ln_g[l, i].reshape(1, D_MODEL), ln_b[l, i].reshape(1, D_MODEL))

        aq, akv, gqkv, gz, gab = _inproj(xt, w_all, l)
        att = _swa(aq, akv, attn_sinks[l], batch, seq)
        gdn = _gdn(gqkv, gab, gz, conv_w[l], gate_params, gdn_norm_g[l].reshape(1, GDN_HEAD_DIM), batch, seq)
        k_mem, v_mem = _memkv(memt, w_kv, l)
        xt = _mix_xattn(att, gdn, xt, w_mix, wq, k_mem, v_mem, wo, ln(0), ln(1), l, batch, seq)
        xt = _mlp_ln(xt, w1, w2, *ln(2), l)
    return xt.reshape(batch, seq, D_MODEL)
```

```python
import functools

import jax
import jax.numpy as jnp
from jax import lax
from jax.experimental import pallas as pl
from jax.experimental.pallas import tpu as pltpu

F32 = jnp.float32
BF16 = jnp.bfloat16

D_MODEL = 1024
DEPTH = 2
ATT_HEADS = 8
ATT_KV_HEADS = 2
ATT_HEAD_DIM = 64
ATT_GROUP = ATT_HEADS // ATT_KV_HEADS
WINDOW = 128
ATT_Q = ATT_HEADS * ATT_HEAD_DIM
ATT_KV = ATT_KV_HEADS * ATT_HEAD_DIM
GDN_HEADS = 4
GDN_HEAD_DIM = 128
GDN_W = GDN_HEADS * GDN_HEAD_DIM
CONV_K = 4
X_HEADS = 4
X_HEAD_DIM = D_MODEL // X_HEADS
D_FF = 4 * D_MODEL
ALPHA = (2 * DEPTH) ** 0.25
LN_EPS = 1e-5
RMS_EPS = 1e-6
NEG_INF = -1e30
ATT_SCALE = ATT_HEAD_DIM ** -0.5
X_SCALE = X_HEAD_DIM ** -0.5
ALIBI_SLOPES = tuple(2.0 ** (-8.0 * (h + 1) / ATT_HEADS) for h in range(ATT_HEADS))
ATT_HEAD_ORDER = tuple(h for p in range(ATT_GROUP) for h in (p, p + ATT_GROUP))

LANES = 128
SUBLANES = 8
ROW_TILE = 512
SUB_ROWS = 256
ATT_ROWS = 512
GDN_CHUNK = 128
GDN_STEP_CHUNKS = 4
FF_CHUNK = 512
VMEM_LIMIT = 56 * 1024 * 1024

_C_AQ = (0, ATT_Q)
_C_AKV = (_C_AQ[1], _C_AQ[1] + 2 * LANES)
_C_GQKV = (_C_AKV[1], _C_AKV[1] + 3 * GDN_W)
_C_GZ = (_C_GQKV[1], _C_GQKV[1] + GDN_W)
_C_GAB = (_C_GZ[1], _C_GZ[1] + LANES)
N_IN = _C_GAB[1]


def _cparams(*sem):
    return pltpu.CompilerParams(dimension_semantics=sem, vmem_limit_bytes=VMEM_LIMIT)


def _resident(shape):
    nd = len(shape)
    return pl.BlockSpec(shape, lambda *_: (0,) * nd, pipeline_mode=pl.Buffered(1))


def _layer_weight(layer, shape):
    nd = len(shape)
    return pl.BlockSpec((None,) + tuple(shape), lambda *_: (layer,) + (0,) * nd, pipeline_mode=pl.Buffered(1))


def _dot(a, b):
    return jnp.dot(a, b, preferred_element_type=F32)


def _dot_nt(a, b):
    return lax.dot_general(a, b, (((1,), (1,)), ((), ())), preferred_element_type=F32)


def _layer_norm(xf, g, b):
    mu = jnp.mean(xf, axis=-1, keepdims=True)
    d = xf - mu
    var = jnp.mean(d * d, axis=-1, keepdims=True)
    return d * lax.rsqrt(var + LN_EPS) * g + b


def _silu(x):
    return x * jax.nn.sigmoid(x)


def _inproj_body(x_ref, w_ref, aq_ref, akv_ref, gqkv_ref, gz_ref, gab_ref):
    xb = x_ref[...].astype(BF16)

    def mm(lo, hi):
        return _dot(xb, w_ref[:, lo:hi])

    aq_ref[...] = mm(*_C_AQ).astype(BF16)
    akv_ref[...] = mm(*_C_AKV).astype(BF16)
    for c in range(3):
        lo = _C_GQKV[0] + c * GDN_W
        gqkv_ref[:, c * GDN_W:(c + 1) * GDN_W] = mm(lo, lo + GDN_W)
    gz_ref[...] = mm(*_C_GZ)
    gab_ref[...] = mm(*_C_GAB)


def _inproj(x, w_all, layer):
    t = x.shape[0]
    row = lambda n: pl.BlockSpec((ROW_TILE, n), lambda i: (i, 0))
    return pl.pallas_call(
        _inproj_body,
        grid=(t // ROW_TILE,),
        in_specs=[row(D_MODEL), _layer_weight(layer, (D_MODEL, N_IN))],
        out_specs=[row(ATT_Q), row(2 * LANES), row(3 * GDN_W), row(GDN_W), row(LANES)],
        out_shape=[jax.ShapeDtypeStruct((t, ATT_Q), BF16),
                   jax.ShapeDtypeStruct((t, 2 * LANES), BF16),
                   jax.ShapeDtypeStruct((t, 3 * GDN_W), F32),
                   jax.ShapeDtypeStruct((t, GDN_W), F32),
                   jax.ShapeDtypeStruct((t, LANES), F32)],
        compiler_params=_cparams("parallel"),
        name="in_proj",
    )(x, w_all)


def _swa_body(sinks_ref, q_ref, kvc_ref, kvp_ref, o_ref):
    first = pl.program_id(1) == 0
    lane = lax.broadcasted_iota(jnp.int32, (1, LANES), 1)
    q_masks = (jnp.where(lane < ATT_HEAD_DIM, ATT_SCALE, 0.0).astype(BF16),
               jnp.where(lane >= ATT_HEAD_DIM, ATT_SCALE, 0.0).astype(BF16))
    low_half = lax.broadcasted_iota(jnp.int32, (WINDOW, LANES), 1) < ATT_HEAD_DIM
    qi = lax.broadcasted_iota(jnp.int32, (WINDOW, WINDOW), 0)
    kj = lax.broadcasted_iota(jnp.int32, (WINDOW, WINDOW), 1)
    in_cur = kj <= qi
    distf = jnp.where(in_cur, qi - kj, qi + WINDOW - kj).astype(F32)
    visible_first = in_cur | (kj >= jnp.where(first, WINDOW, 0))

    n_blk = ATT_ROWS // WINDOW
    rows = [slice(blk * WINDOW, (blk + 1) * WINDOW) for blk in range(n_blk)]

    def band(blk):
        prev = kvp_ref[...] if blk == 0 else kvc_ref[rows[blk - 1], :]
        return jnp.concatenate([prev, kvc_ref[rows[blk], :]], axis=0)

    def scores(blk):
        kk = band(blk)[:, 0:LANES]
        out = []
        for pos in range(ATT_HEADS):
            qp = q_ref[rows[blk], (pos // 2) * LANES:(pos // 2 + 1) * LANES]
            out.append(_dot_nt(qp * q_masks[pos % 2], kk))
        return out

    def attend(blk, s2_heads):
        vv = band(blk)[:, LANES:2 * LANES]
        halves = []
        for pos, s2 in enumerate(s2_heads):
            hq = ATT_HEAD_ORDER[pos]
            s = jnp.where(in_cur, s2[:, WINDOW:], s2[:, :WINDOW]) - ALIBI_SLOPES[hq] * distf
            if blk == 0:
                s = jnp.where(visible_first, s, NEG_INF)
            sink = sinks_ref[hq]
            m = jnp.maximum(jnp.max(s, axis=-1, keepdims=True), sink)
            p = jnp.exp(s - m)
            denom = jnp.sum(p, axis=-1, keepdims=True) + jnp.exp(sink - m)
            pn = p * (1.0 / denom)
            p2 = jnp.concatenate([jnp.where(in_cur, 0.0, pn), jnp.where(in_cur, pn, 0.0)], axis=1)
            halves.append(_dot(p2.astype(BF16), vv))
        for pair in range(ATT_HEADS // 2):
            o_ref[rows[blk], pair * LANES:(pair + 1) * LANES] = jnp.where(
                low_half, halves[2 * pair], halves[2 * pair + 1]).astype(BF16)

    s2_heads = scores(0)
    for blk in range(n_blk):
        s2_next = scores(blk + 1) if blk + 1 < n_blk else None
        attend(blk, s2_heads)
        s2_heads = s2_next


def _swa(aq, akv, sinks, batch, seq):
    t = aq.shape[0]
    steps = seq // ATT_ROWS
    blocks_per_step = ATT_ROWS // WINDOW
    blocks_per_seq = seq // WINDOW

    def prev_map(b, j):
        return (b * blocks_per_seq + jnp.maximum(j * blocks_per_step - 1, 0), 0)

    return pl.pallas_call(
        _swa_body,
        grid=(batch, steps),
        in_specs=[pl.BlockSpec(memory_space=pltpu.SMEM),
                  pl.BlockSpec((ATT_ROWS, ATT_Q), lambda b, j: (b * steps + j, 0)),
                  pl.BlockSpec((ATT_ROWS, 2 * LANES), lambda b, j: (b * steps + j, 0)),
                  pl.BlockSpec((WINDOW, 2 * LANES), prev_map)],
        out_specs=pl.BlockSpec((ATT_ROWS, ATT_Q), lambda b, j: (b * steps + j, 0)),
        out_shape=jax.ShapeDtypeStruct((t, ATT_Q), BF16),
        compiler_params=_cparams("parallel", "parallel"),
        name="swa_attention",
    )(sinks, aq, akv, akv)


def _unit_lower_inverse_stages(mats, ii, jj, out):
    base = 16

    def same_block(size):
        return (ii // size) == (jj // size)

    def bf(ms):
        return [m.astype(BF16) for m in ms]

    eye = (ii == jj).astype(F32)
    in_base = same_block(base)
    b1 = [jnp.where(in_base, -a, 0.0) for a in mats]
    b1_b = bf(b1)
    b2_b = bf([_dot(b, b) for b in b1_b])
    yield
    t = [eye + b for b in b1]
    b4 = [_dot(b, b) for b in b2_b]
    t = [x + _dot(xb, p) for x, xb, p in zip(t, bf(t), b2_b)]
    b4_b = bf(b4)
    yield
    b8 = [_dot(b, b) for b in b4_b]
    t = [x + _dot(xb, p) for x, xb, p in zip(t, bf(t), b4_b)]
    b8_b = bf(b8)
    yield
    t = [x + _dot(xb, p) for x, xb, p in zip(t, bf(t), b8_b)]
    yield
    size = base
    while size < GDN_CHUNK:
        off_diag = same_block(2 * size) & jnp.logical_not(same_block(size))
        t_b = bf(t)
        ct_b = bf([_dot(jnp.where(off_diag, a, 0.0).astype(BF16), xb) for a, xb in zip(mats, t_b)])
        yield
        t = [x - _dot(xb, y) for x, xb, y in zip(t, t_b, ct_b)]
        yield
        size *= 2
    out.extend(t)


def _gdn_body(gp_ref, convw_ref, normg_ref, x_ref, xprev_ref, gab_ref, z_ref, o_ref,
              xs_ref, a_ref, rhs_ref, qd_ref, kdt_ref, intra_ref, cd_ref, state_ref, qkv_ref,
              *, groups_per_seq, n_groups):
    step = pl.program_id(0)
    c_rows = GDN_CHUNK
    subs = range(GDN_STEP_CHUNKS)
    heads = range(GDN_HEADS)
    pairs = [(j, h) for j in subs for h in heads]
    slot = lambda j, h: j * GDN_HEADS + h
    head_cols = [slice(h * GDN_HEAD_DIM, (h + 1) * GDN_HEAD_DIM) for h in heads]
    chunk_rows = [slice(j * c_rows, (j + 1) * c_rows) for j in subs]

    @pl.when(step == 0)
    def _():
        for ref in (a_ref, rhs_ref, qd_ref, kdt_ref, intra_ref, cd_ref, state_ref):
            ref[...] = jnp.zeros_like(ref)

    ii = lax.broadcasted_iota(jnp.int32, (c_rows, c_rows), 0)
    jj = lax.broadcasted_iota(jnp.int32, (c_rows, c_rows), 1)
    causal = jj <= ii
    strict = jj < ii

    def finish():
        t_inv = []
        yield from _unit_lower_inverse_stages([a_ref[slot(j, h)] for j, h in pairs], ii, jj, t_inv)
        uw = [_dot(t.astype(BF16), rhs_ref[slot(j, h)]) for (j, h), t in zip(pairs, t_inv)]
        yield
        keep = jnp.where(lax.rem(step - 1, groups_per_seq) == 0, 0.0, 1.0)
        state = [state_ref[h] * keep for h in heads]
        for j in subs:
            r = [_dot(jnp.concatenate([uw[slot(j, h)][:, GDN_HEAD_DIM:].astype(BF16),
                                       qd_ref[j, :, head_cols[h]]], axis=0),
                      state[h].astype(BF16)) for h in heads]
            yield
            v_new_b = [(uw[slot(j, h)][:, :GDN_HEAD_DIM] - r[h][:c_rows]).astype(BF16) for h in heads]
            state = [state[h] * cd_ref[j, 0:1, h:h + 1] + _dot(kdt_ref[j, :, head_cols[h]], v_new_b[h])
                     for h in heads]
            yield
            for h in heads:
                o = r[h][c_rows:] + _dot(intra_ref[j, :, head_cols[h]], v_new_b[h])
                o = (o * lax.rsqrt(jnp.mean(o * o, axis=-1, keepdims=True) + RMS_EPS)
                     * normg_ref[...] * _silu(z_ref[chunk_rows[j], head_cols[h]]))
                o_ref[chunk_rows[j], head_cols[h]] = o.astype(BF16)
            yield
        for h in heads:
            state_ref[h] = state[h]

    first = lax.rem(jnp.minimum(step, n_groups - 1), groups_per_seq) == 0
    xs_ref[0:SUBLANES, :] = jnp.where(first, 0.0, xprev_ref[...])
    xs_ref[SUBLANES:SUBLANES + GDN_STEP_CHUNKS * c_rows, :] = x_ref[...]

    def convolve():
        for j, h in pairs:
            for part in range(3):
                col = part * GDN_W + h * GDN_HEAD_DIM
                window = xs_ref[j * c_rows:j * c_rows + SUBLANES + c_rows, col:col + LANES]
                acc = None
                for tap in range(CONV_K):
                    shift = CONV_K - 1 - tap
                    shifted = pltpu.roll(window, shift, axis=0) if shift else window
                    term = convw_ref[tap:tap + 1, col:col + LANES] * shifted[SUBLANES:, :]
                    acc = term if acc is None else acc + term
                y = _silu(acc)
                if part < 2:
                    y = y * lax.rsqrt(jnp.sum(y * y, axis=-1, keepdims=True) + RMS_EPS)
                qkv_ref[3 * slot(j, h) + part] = y
                yield

    finishing, convolving = finish(), convolve()
    n_finish, n_conv = 11 + 3 * GDN_STEP_CHUNKS, 3 * len(pairs)
    done = 0
    for turn, _ in enumerate(finishing):
        while done * n_finish < (turn + 1) * n_conv and next(convolving, True) is None:
            done += 1
    for _ in convolving:
        pass

    a_log = gp_ref[0:1, :]
    dt_bias = gp_ref[1:2, :]
    ltri = causal.astype(F32)
    for j in subs:
        gab = gab_ref[chunk_rows[j], :]
        g_all = -jnp.exp(a_log) * jax.nn.softplus(gab + dt_bias)
        beta_all = jax.nn.sigmoid(gab)
        decay_cols = jnp.dot(ltri, g_all, precision=lax.Precision.HIGHEST, preferred_element_type=F32)
        decay_rows = lax.dot_general(g_all.T[0:SUBLANES, :], ltri, (((1,), (1,)), ((), ())),
                                     precision=lax.Precision.HIGHEST, preferred_element_type=F32)
        decay_last = decay_cols[c_rows - 1:c_rows, :]
        cd_ref[j] = jnp.broadcast_to(jnp.exp(decay_last), (SUBLANES, LANES))
        for h in heads:
            cols = head_cols[h]
            qn = qkv_ref[3 * slot(j, h)] * (GDN_HEAD_DIM ** -0.5)
            kn = qkv_ref[3 * slot(j, h) + 1]
            v = qkv_ref[3 * slot(j, h) + 2]
            dcol = decay_cols[:, h:h + 1]
            drow = decay_rows[h:h + 1, :]
            lmask = jnp.where(causal, jnp.exp(jnp.where(causal, dcol - drow, 0.0)), 0.0)
            e_dec = jnp.exp(dcol)
            e_rem = jnp.exp(decay_last[:, h:h + 1] - dcol)
            beta = beta_all[:, GDN_HEADS + h:GDN_HEADS + h + 1]
            kbeta = kn * beta
            kn_b = kn.astype(BF16)
            a_ref[slot(j, h)] = jnp.where(strict, _dot_nt(kbeta.astype(BF16), kn_b) * lmask, 0.0)
            intra_ref[j, :, cols] = jnp.where(causal, _dot_nt(qn.astype(BF16), kn_b) * lmask, 0.0).astype(BF16)
            rhs_ref[slot(j, h)] = jnp.concatenate([v * beta, kbeta * e_dec], axis=1).astype(BF16)
            qd_ref[j, :, cols] = (qn * e_dec).astype(BF16)
            kdt_ref[j, :, cols] = (kn * e_rem).T.astype(BF16)


def _gdn(gqkv, gab, gz, conv_w, gate_params, norm_g, batch, seq):
    t = gqkv.shape[0]
    group = GDN_STEP_CHUNKS * GDN_CHUNK
    groups_per_seq = seq // group
    n_groups = batch * groups_per_seq
    per8 = group // SUBLANES
    n_slots = GDN_STEP_CHUNKS * GDN_HEADS
    cur = lambda s: jnp.minimum(s, n_groups - 1)
    done = lambda s: jnp.maximum(s - 1, 0)
    vmem = lambda shape, dt: pltpu.VMEM(shape, dt)
    return pl.pallas_call(
        functools.partial(_gdn_body, groups_per_seq=groups_per_seq, n_groups=n_groups),
        grid=(n_groups + 1,),
        in_specs=[_resident((SUBLANES, LANES)), _resident((CONV_K, 3 * GDN_W)),
                  _resident((1, GDN_HEAD_DIM)),
                  pl.BlockSpec((group, 3 * GDN_W), lambda s: (cur(s), 0)),
                  pl.BlockSpec((SUBLANES, 3 * GDN_W), lambda s: (jnp.maximum(cur(s) * per8 - 1, 0), 0)),
                  pl.BlockSpec((group, LANES), lambda s: (cur(s), 0)),
                  pl.BlockSpec((group, GDN_W), lambda s: (done(s), 0))],
        out_specs=pl.BlockSpec((group, GDN_W), lambda s: (done(s), 0)),
        out_shape=jax.ShapeDtypeStruct((t, GDN_W), BF16),
        scratch_shapes=[vmem((SUBLANES + group, 3 * GDN_W), F32),
                        vmem((n_slots, GDN_CHUNK, GDN_CHUNK), F32),
                        vmem((n_slots, GDN_CHUNK, 2 * GDN_HEAD_DIM), BF16),
                        vmem((GDN_STEP_CHUNKS, GDN_CHUNK, GDN_W), BF16),
                        vmem((GDN_STEP_CHUNKS, GDN_HEAD_DIM, GDN_HEADS * GDN_CHUNK), BF16),
                        vmem((GDN_STEP_CHUNKS, GDN_CHUNK, GDN_HEADS * GDN_CHUNK), BF16),
                        vmem((GDN_STEP_CHUNKS, SUBLANES, LANES), F32),
                        vmem((GDN_HEADS, GDN_HEAD_DIM, GDN_HEAD_DIM), F32),
                        vmem((3 * n_slots, GDN_CHUNK, GDN_HEAD_DIM), F32)],
        compiler_params=_cparams("arbitrary"),
        name="gdn",
    )(gate_params, conv_w, norm_g, gqkv, gqkv, gab, gz)


def _memkv_body(m_ref, w_ref, k_ref, v_ref):
    mb = m_ref[...].astype(BF16)
    k_ref[...] = _dot(mb, w_ref[:, 0:D_MODEL]).astype(BF16)
    v_ref[...] = _dot(mb, w_ref[:, D_MODEL:2 * D_MODEL]).astype(BF16)


def _memkv(mem, w_kv, layer):
    rows = mem.shape[0]
    tile = min(ROW_TILE, rows)
    row = pl.BlockSpec((tile, D_MODEL), lambda i: (i, 0))
    return pl.pallas_call(
        _memkv_body,
        grid=(rows // tile,),
        in_specs=[row, _layer_weight(layer, (D_MODEL, 2 * D_MODEL))],
        out_specs=[row, row],
        out_shape=[jax.ShapeDtypeStruct((rows, D_MODEL), BF16)] * 2,
        compiler_params=_cparams("parallel"),
        name="mem_kv_proj",
    )(mem, w_kv)


def _mix_xattn_body(att_ref, gdn_ref, x_ref, wmix_ref, wq_ref, k_ref, v_ref, wo_ref,
                    g0_ref, b0_ref, g1_ref, b1_ref, o_ref):
    n_sub = ROW_TILE // SUB_ROWS
    rows = [slice(r * SUB_ROWS, (r + 1) * SUB_ROWS) for r in range(n_sub)]
    head_cols = [slice(h * X_HEAD_DIM, (h + 1) * X_HEAD_DIM) for h in range(X_HEADS)]

    def mix(r):
        y = (_dot(att_ref[rows[r], :], wmix_ref[0:ATT_Q, :])
             + _dot(gdn_ref[rows[r], :], wmix_ref[ATT_Q:ATT_Q + GDN_W, :]))
        return _layer_norm(ALPHA * x_ref[rows[r], :] + y, g0_ref[...], b0_ref[...])

    def project_q(x1):
        return (_dot(x1.astype(BF16), wq_ref[...]) * X_SCALE).astype(BF16)

    def scores(q):
        return [_dot_nt(q[:, c], k_ref[:, c]) for c in head_cols]

    def softmax(s_heads):
        out = []
        for s in s_heads:
            p = jnp.exp(s - jnp.max(s, axis=-1, keepdims=True))
            out.append((p * (1.0 / jnp.sum(p, axis=-1, keepdims=True))).astype(BF16))
        return out

    def attend(p_heads):
        return jnp.concatenate([_dot(p, v_ref[:, c]).astype(BF16) for p, c in zip(p_heads, head_cols)], axis=1)

    def finish(r, x1, heads):
        y = _dot(heads, wo_ref[...])
        o_ref[rows[r], :] = _layer_norm(ALPHA * x1 + y, g1_ref[...], b1_ref[...])

    x1 = [mix(r) for r in range(n_sub)]
    s_prev = scores(project_q(x1[0]))
    for r in range(n_sub):
        q_next = project_q(x1[r + 1]) if r + 1 < n_sub else None
        heads = attend(softmax(s_prev))
        if q_next is not None:
            s_prev = scores(q_next)
        finish(r, x1[r], heads)


def _mix_xattn(att, gdn, x, w_mix, wq, k_mem, v_mem, wo, ln0, ln1, layer, batch, seq):
    t = x.shape[0]
    steps = seq // ROW_TILE
    mem_len = k_mem.shape[0] // batch
    row = lambda n: pl.BlockSpec((ROW_TILE, n), lambda bb, i: (bb * steps + i, 0))
    mem = pl.BlockSpec((mem_len, D_MODEL), lambda bb, i: (bb, 0))
    weight = _layer_weight(layer, (D_MODEL, D_MODEL))
    vec = _resident((1, D_MODEL))
    return pl.pallas_call(
        _mix_xattn_body,
        grid=(batch, steps),
        in_specs=[row(ATT_Q), row(GDN_W), row(D_MODEL), weight, weight, mem, mem, weight, vec, vec, vec, vec],
        out_specs=row(D_MODEL),
        out_shape=jax.ShapeDtypeStruct((t, D_MODEL), F32),
        compiler_params=_cparams("parallel", "parallel"),
        name="mix_xattn_ln",
    )(att, gdn, x, w_mix, wq, k_mem, v_mem, wo, *ln0, *ln1)


def _mlp_ln_body(x_ref, w1_ref, w2_ref, g_ref, b_ref, o_ref):
    x = x_ref[...]
    xb = x.astype(BF16)
    n_chunks = D_FF // FF_CHUNK

    def hidden(c):
        cols = slice(c * FF_CHUNK, (c + 1) * FF_CHUNK)
        return jnp.square(jnp.maximum(_dot(xb, w1_ref[:, cols]), 0.0)).astype(BF16), cols

    y = None
    for c in range(n_chunks - 1):
        hid, cols = hidden(c)
        part = _dot(hid, w2_ref[cols, :])
        y = part if y is None else y + part
    hid, cols = hidden(n_chunks - 1)
    for r in range(ROW_TILE // SUB_ROWS):
        rows = slice(r * SUB_ROWS, (r + 1) * SUB_ROWS)
        y_r = y[rows, :] + _dot(hid[rows, :], w2_ref[cols, :])
        o_ref[rows, :] = _layer_norm(ALPHA * x[rows, :] + y_r, g_ref[...], b_ref[...])


def _mlp_ln(x, w1, w2, g, b, layer):
    t = x.shape[0]
    row = pl.BlockSpec((ROW_TILE, D_MODEL), lambda i: (i, 0))
    return pl.pallas_call(
        _mlp_ln_body,
        grid=(t // ROW_TILE,),
        in_specs=[row, _layer_weight(layer, (D_MODEL, D_FF)), _layer_weight(layer, (D_FF, D_MODEL)),
                  _resident((1, D_MODEL)), _resident((1, D_MODEL))],
        out_specs=row,
        out_shape=jax.ShapeDtypeStruct((t, D_MODEL), F32),
        compiler_params=_cparams("parallel"),
        name="mlp_ln",
    )(x, w1, w2, g, b)


def _pack_w_in(w_in):
    w = w_in.astype(BF16)
    o = 0
    aq = w[..., o:o + ATT_Q]; o += ATT_Q
    ak = w[..., o:o + ATT_KV]; o += ATT_KV
    av = w[..., o:o + ATT_KV]; o += ATT_KV
    gqkv = w[..., o:o + 3 * GDN_W]; o += 3 * GDN_W
    ga = w[..., o:o + GDN_HEADS]; o += GDN_HEADS
    gb = w[..., o:o + GDN_HEADS]; o += GDN_HEADS
    gz = w[..., o:o + GDN_W]
    aq = jnp.concatenate([aq[..., h * ATT_HEAD_DIM:(h + 1) * ATT_HEAD_DIM] for h in ATT_HEAD_ORDER], axis=-1)
    pad = jnp.zeros(w.shape[:-1] + (LANES - 2 * GDN_HEADS,), BF16)
    return jnp.concatenate([aq, ak, av, gqkv, gz, ga, gb, pad], axis=-1)


def _pack_w_mix(w_mix_out):
    w = w_mix_out.astype(BF16)
    att = jnp.concatenate([w[:, h * ATT_HEAD_DIM:(h + 1) * ATT_HEAD_DIM, :] for h in ATT_HEAD_ORDER], axis=1)
    return jnp.concatenate([att, w[:, ATT_Q:, :]], axis=1)


def _pad_lanes(v):
    return jnp.pad(v, (0, LANES - v.shape[0]))


def kernel(x, mem, w_in, conv_w, attn_sinks, a_log, dt_bias, gdn_norm_g, w_mix_out,
           wq_mem, wk_mem, wv_mem, wo_mem, w_ff1, w_ff2, ln_g, ln_b):
    batch, seq, _ = x.shape
    xt = x.reshape(batch * seq, D_MODEL)
    memt = mem.reshape(batch * mem.shape[1], D_MODEL)
    w_all = _pack_w_in(w_in)
    w_mix, wq, wo = _pack_w_mix(w_mix_out), wq_mem.astype(BF16), wo_mem.astype(BF16)
    w_kv = jnp.concatenate([wk_mem.astype(BF16), wv_mem.astype(BF16)], axis=-1)
    w1, w2 = w_ff1.astype(BF16), w_ff2.astype(BF16)
    for l in range(DEPTH):
        gate_params = jnp.zeros((SUBLANES, LANES), F32)
        gate_params = gate_params.at[0].set(_pad_lanes(a_log[l])).at[1].set(_pad_lanes(dt_bias[l]))
        ln = lambda i: (ln_g[l, i].reshape(1, D_MODEL), ln_b[l, i].reshape(1, D_MODEL))

        aq, akv, gqkv, gz, gab = _inproj(xt, w_all, l)
        att = _swa(aq, akv, attn_sinks[l], batch, seq)
        gdn = _gdn(gqkv, gab, gz, conv_w[l], gate_params, gdn_norm_g[l].reshape(1, GDN_HEAD_DIM), batch, seq)
        k_mem, v_mem = _memkv(memt, w_kv, l)
        xt = _mix_xattn(att, gdn, xt, w_mix, wq, k_mem, v_mem, wo, ln(0), ln(1), l, batch, seq)
        xt = _mlp_ln(xt, w1, w2, *ln(2), l)
    return xt.reshape(batch, seq, D_MODEL)
```

```python
import functools

import jax
import jax.numpy as jnp
from jax import lax
from jax.experimental import pallas as pl
from jax.experimental.pallas import tpu as pltpu

F32 = jnp.float32
BF16 = jnp.bfloat16

D_MODEL = 1024
DEPTH = 2
ATT_HEADS = 8
ATT_KV_HEADS = 2
ATT_HEAD_DIM = 64
ATT_GROUP = ATT_HEADS // ATT_KV_HEADS
WINDOW = 128
ATT_Q = ATT_HEADS * ATT_HEAD_DIM
ATT_KV = ATT_KV_HEADS * ATT_HEAD_DIM
GDN_HEADS = 4
GDN_HEAD_DIM = 128
GDN_W = GDN_HEADS * GDN_HEAD_DIM
CONV_K = 4
X_HEADS = 4
X_HEAD_DIM = D_MODEL // X_HEADS
D_FF = 4 * D_MODEL
ALPHA = (2 * DEPTH) ** 0.25
LN_EPS = 1e-5
RMS_EPS = 1e-6
NEG_INF = -1e30
ATT_SCALE = ATT_HEAD_DIM ** -0.5
X_SCALE = X_HEAD_DIM ** -0.5
ALIBI_SLOPES = tuple(2.0 ** (-8.0 * (h + 1) / ATT_HEADS) for h in range(ATT_HEADS))
ATT_HEAD_ORDER = tuple(h for p in range(ATT_GROUP) for h in (p, p + ATT_GROUP))

LANES = 128
SUBLANES = 8
ROW_TILE = 512
SUB_ROWS = 256
ATT_ROWS = 512
GDN_CHUNK = 128
GDN_STEP_CHUNKS = 4
FF_CHUNK = 512
VMEM_LIMIT = 56 * 1024 * 1024

_C_AQ = (0, ATT_Q)
_C_AKV = (_C_AQ[1], _C_AQ[1] + 2 * LANES)
_C_GQKV = (_C_AKV[1], _C_AKV[1] + 3 * GDN_W)
_C_GZ = (_C_GQKV[1], _C_GQKV[1] + GDN_W)
_C_GAB = (_C_GZ[1], _C_GZ[1] + LANES)
N_IN = _C_GAB[1]


def _cparams(*sem):
    return pltpu.CompilerParams(dimension_semantics=sem, vmem_limit_bytes=VMEM_LIMIT)


def _resident(shape):
    nd = len(shape)
    return pl.BlockSpec(shape, lambda *_: (0,) * nd, pipeline_mode=pl.Buffered(1))


def _layer_weight(layer, shape):
    nd = len(shape)
    return pl.BlockSpec((None,) + tuple(shape), lambda *_: (layer,) + (0,) * nd, pipeline_mode=pl.Buffered(1))


def _dot(a, b):
    return jnp.dot(a, b, preferred_element_type=F32)


def _dot_nt(a, b):
    return lax.dot_general(a, b, (((1,), (1,)), ((), ())), preferred_element_type=F32)


def _layer_norm(xf, g, b):
    mu = jnp.mean(xf, axis=-1, keepdims=True)
    d = xf - mu
    var = jnp.mean(d * d, axis=-1, keepdims=True)
    return d * lax.rsqrt(var + LN_EPS) * g + b


def _silu(x):
    h = 0.5 * x
    return h + h * jnp.tanh(h)


def _inproj_body(x_ref, w_ref, aq_ref, akv_ref, gqkv_ref, gz_ref, gab_ref):
    xb = x_ref[...].astype(BF16)

    def mm(lo, hi):
        return _dot(xb, w_ref[:, lo:hi])

    aq_ref[...] = mm(*_C_AQ).astype(BF16)
    akv_ref[...] = mm(*_C_AKV).astype(BF16)
    for c in range(3):
        lo = _C_GQKV[0] + c * GDN_W
        gqkv_ref[:, c * GDN_W:(c + 1) * GDN_W] = mm(lo, lo + GDN_W)
    gz_ref[...] = mm(*_C_GZ)
    gab_ref[...] = mm(*_C_GAB)


def _inproj(x, w_all, layer):
    t = x.shape[0]
    row = lambda n: pl.BlockSpec((ROW_TILE, n), lambda i: (i, 0))
    return pl.pallas_call(
        _inproj_body,
        grid=(t // ROW_TILE,),
        in_specs=[row(D_MODEL), _layer_weight(layer, (D_MODEL, N_IN))],
        out_specs=[row(ATT_Q), row(2 * LANES), row(3 * GDN_W), row(GDN_W), row(LANES)],
        out_shape=[jax.ShapeDtypeStruct((t, ATT_Q), BF16),
                   jax.ShapeDtypeStruct((t, 2 * LANES), BF16),
                   jax.ShapeDtypeStruct((t, 3 * GDN_W), F32),
                   jax.ShapeDtypeStruct((t, GDN_W), F32),
                   jax.ShapeDtypeStruct((t, LANES), F32)],
        compiler_params=_cparams("parallel"),
        name="in_proj",
    )(x, w_all)


def _swa_body(sinks_ref, q_ref, kvc_ref, kvp_ref, o_ref):
    first = pl.program_id(1) == 0
    lane = lax.broadcasted_iota(jnp.int32, (1, LANES), 1)
    q_masks = (jnp.where(lane < ATT_HEAD_DIM, ATT_SCALE, 0.0).astype(BF16),
               jnp.where(lane >= ATT_HEAD_DIM, ATT_SCALE, 0.0).astype(BF16))
    low_half = lax.broadcasted_iota(jnp.int32, (WINDOW, LANES), 1) < ATT_HEAD_DIM
    qi = lax.broadcasted_iota(jnp.int32, (WINDOW, WINDOW), 0)
    kj = lax.broadcasted_iota(jnp.int32, (WINDOW, WINDOW), 1)
    in_cur = kj <= qi
    distf = jnp.where(in_cur, qi - kj, qi + WINDOW - kj).astype(F32)
    visible_first = in_cur | (kj >= jnp.where(first, WINDOW, 0))

    n_blk = ATT_ROWS // WINDOW
    rows = [slice(blk * WINDOW, (blk + 1) * WINDOW) for blk in range(n_blk)]

    def band(blk):
        prev = kvp_ref[...] if blk == 0 else kvc_ref[rows[blk - 1], :]
        return jnp.concatenate([prev, kvc_ref[rows[blk], :]], axis=0)

    def scores(blk):
        kk = band(blk)[:, 0:LANES]
        out = []
        for pos in range(ATT_HEADS):
            qp = q_ref[rows[blk], (pos // 2) * LANES:(pos // 2 + 1) * LANES]
            out.append(_dot_nt(qp * q_masks[pos % 2], kk))
        return out

    def attend(blk, s2_heads):
        vv = band(blk)[:, LANES:2 * LANES]
        halves = []
        for pos, s2 in enumerate(s2_heads):
            hq = ATT_HEAD_ORDER[pos]
            s = jnp.where(in_cur, s2[:, WINDOW:], s2[:, :WINDOW]) - ALIBI_SLOPES[hq] * distf
            if blk == 0:
                s = jnp.where(visible_first, s, NEG_INF)
            sink = sinks_ref[hq]
            m = jnp.maximum(jnp.max(s, axis=-1, keepdims=True), sink)
            p = jnp.exp(s - m)
            denom = jnp.sum(p, axis=-1, keepdims=True) + jnp.exp(sink - m)
            pn = p * (1.0 / denom)
            p2 = jnp.concatenate([jnp.where(in_cur, 0.0, pn), jnp.where(in_cur, pn, 0.0)], axis=1)
            halves.append(_dot(p2.astype(BF16), vv))
        for pair in range(ATT_HEADS // 2):
            o_ref[rows[blk], pair * LANES:(pair + 1) * LANES] = jnp.where(
                low_half, halves[2 * pair], halves[2 * pair + 1]).astype(BF16)

    s2_heads = scores(0)
    for blk in range(n_blk):
        s2_next = scores(blk + 1) if blk + 1 < n_blk else None
        attend(blk, s2_heads)
        s2_heads = s2_next


def _swa(aq, akv, sinks, batch, seq):
    t = aq.shape[0]
    steps = seq // ATT_ROWS
    blocks_per_step = ATT_ROWS // WINDOW
    blocks_per_seq = seq // WINDOW

    def prev_map(b, j):
        return (b * blocks_per_seq + jnp.maximum(j * blocks_per_step - 1, 0), 0)

    return pl.pallas_call(
        _swa_body,
        grid=(batch, steps),
        in_specs=[pl.BlockSpec(memory_space=pltpu.SMEM),
                  pl.BlockSpec((ATT_ROWS, ATT_Q), lambda b, j: (b * steps + j, 0)),
                  pl.BlockSpec((ATT_ROWS, 2 * LANES), lambda b, j: (b * steps + j, 0)),
                  pl.BlockSpec((WINDOW, 2 * LANES), prev_map)],
        out_specs=pl.BlockSpec((ATT_ROWS, ATT_Q), lambda b, j: (b * steps + j, 0)),
        out_shape=jax.ShapeDtypeStruct((t, ATT_Q), BF16),
        compiler_params=_cparams("parallel", "parallel"),
        name="swa_attention",
    )(sinks, aq, akv, akv)


def _unit_lower_inverse_stages(mats, ii, jj, out):
    base = 16

    def same_block(size):
        return (ii // size) == (jj // size)

    def bf(ms):
        return [m.astype(BF16) for m in ms]

    eye = (ii == jj).astype(F32)
    in_base = same_block(base)
    b1 = [jnp.where(in_base, -a, 0.0) for a in mats]
    b1_b = bf(b1)
    b2_b = bf([_dot(b, b) for b in b1_b])
    yield
    t = [eye + b for b in b1]
    b4 = [_dot(b, b) for b in b2_b]
    t = [x + _dot(xb, p) for x, xb, p in zip(t, bf(t), b2_b)]
    b4_b = bf(b4)
    yield
    b8 = [_dot(b, b) for b in b4_b]
    t = [x + _dot(xb, p) for x, xb, p in zip(t, bf(t), b4_b)]
    b8_b = bf(b8)
    yield
    t = [x + _dot(xb, p) for x, xb, p in zip(t, bf(t), b8_b)]
    yield
    size = base
    while size < GDN_CHUNK:
        off_diag = same_block(2 * size) & jnp.logical_not(same_block(size))
        t_b = bf(t)
        ct_b = bf([_dot(jnp.where(off_diag, a, 0.0).astype(BF16), xb) for a, xb in zip(mats, t_b)])
        yield
        t = [x - _dot(xb, y) for x, xb, y in zip(t, t_b, ct_b)]
        yield
        size *= 2
    out.extend(t)


def _gdn_body(gp_ref, convw_ref, normg_ref, x_ref, xprev_ref, gab_ref, z_ref, o_ref,
              xs_ref, a_ref, rhs_ref, qd_ref, kdt_ref, intra_ref, cd_ref, state_ref, qkv_ref,
              *, groups_per_seq, n_groups):
    step = pl.program_id(0)
    c_rows = GDN_CHUNK
    subs = range(GDN_STEP_CHUNKS)
    heads = range(GDN_HEADS)
    pairs = [(j, h) for j in subs for h in heads]
    slot = lambda j, h: j * GDN_HEADS + h
    head_cols = [slice(h * GDN_HEAD_DIM, (h + 1) * GDN_HEAD_DIM) for h in heads]
    chunk_rows = [slice(j * c_rows, (j + 1) * c_rows) for j in subs]

    @pl.when(step == 0)
    def _():
        for ref in (a_ref, rhs_ref, qd_ref, kdt_ref, intra_ref, cd_ref, state_ref):
            ref[...] = jnp.zeros_like(ref)

    ii = lax.broadcasted_iota(jnp.int32, (c_rows, c_rows), 0)
    jj = lax.broadcasted_iota(jnp.int32, (c_rows, c_rows), 1)
    causal = jj <= ii
    strict = jj < ii

    def finish():
        t_inv = []
        yield from _unit_lower_inverse_stages([a_ref[slot(j, h)] for j, h in pairs], ii, jj, t_inv)
        uw = [_dot(t.astype(BF16), rhs_ref[slot(j, h)]) for (j, h), t in zip(pairs, t_inv)]
        yield
        keep = jnp.where(lax.rem(step - 1, groups_per_seq) == 0, 0.0, 1.0)
        state = [state_ref[h] * keep for h in heads]
        for j in subs:
            r = [_dot(jnp.concatenate([uw[slot(j, h)][:, GDN_HEAD_DIM:].astype(BF16),
                                       qd_ref[j, :, head_cols[h]]], axis=0),
                      state[h].astype(BF16)) for h in heads]
            yield
            v_new_b = [(uw[slot(j, h)][:, :GDN_HEAD_DIM] - r[h][:c_rows]).astype(BF16) for h in heads]
            state = [state[h] * cd_ref[j, 0:1, h:h + 1] + _dot(kdt_ref[j, :, head_cols[h]], v_new_b[h])
                     for h in heads]
            yield
            for h in heads:
                o = r[h][c_rows:] + _dot(intra_ref[j, :, head_cols[h]], v_new_b[h])
                o = (o * lax.rsqrt(jnp.mean(o * o, axis=-1, keepdims=True) + RMS_EPS)
                     * normg_ref[...] * _silu(z_ref[chunk_rows[j], head_cols[h]]))
                o_ref[chunk_rows[j], head_cols[h]] = o.astype(BF16)
            yield
        for h in heads:
            state_ref[h] = state[h]

    first = lax.rem(jnp.minimum(step, n_groups - 1), groups_per_seq) == 0
    xs_ref[0:SUBLANES, :] = jnp.where(first, 0.0, xprev_ref[...])
    xs_ref[SUBLANES:SUBLANES + GDN_STEP_CHUNKS * c_rows, :] = x_ref[...]

    def convolve():
        for j, h in pairs:
            for part in range(3):
                col = part * GDN_W + h * GDN_HEAD_DIM
                window = xs_ref[j * c_rows:j * c_rows + SUBLANES + c_rows, col:col + LANES]
                w0, w1, w2, w3 = (convw_ref[tap:tap + 1, col:col + LANES] for tap in range(CONV_K))
                x1 = pltpu.roll(window, 1, axis=0)
                u = w1 * window + w0 * x1
                acc = (w3 * window + w2 * x1 + pltpu.roll(u, 2, axis=0))[SUBLANES:, :]
                y = _silu(acc)
                if part < 2:
                    y = y * lax.rsqrt(jnp.sum(y * y, axis=-1, keepdims=True) + RMS_EPS)
                qkv_ref[3 * slot(j, h) + part] = y
                yield

    finishing, convolving = finish(), convolve()
    n_finish, n_conv = 11 + 3 * GDN_STEP_CHUNKS, 3 * len(pairs)
    done = 0
    for turn, _ in enumerate(finishing):
        while done * n_finish < (turn + 1) * n_conv and next(convolving, True) is None:
            done += 1
    for _ in convolving:
        pass

    a_log = gp_ref[0:1, :]
    dt_bias = gp_ref[1:2, :]
    ltri = causal.astype(F32)
    for j in subs:
        gab = gab_ref[chunk_rows[j], :]
        g_all = -jnp.exp(a_log) * jax.nn.softplus(gab + dt_bias)
        beta_all = jax.nn.sigmoid(gab)
        decay_cols = jnp.dot(ltri, g_all, precision=lax.Precision.HIGHEST, preferred_element_type=F32)
        decay_rows = lax.dot_general(g_all.T[0:SUBLANES, :], ltri, (((1,), (1,)), ((), ())),
                                     precision=lax.Precision.HIGHEST, preferred_element_type=F32)
        decay_last = decay_cols[c_rows - 1:c_rows, :]
        cd_ref[j] = jnp.broadcast_to(jnp.exp(decay_last), (SUBLANES, LANES))
        for h in heads:
            cols = head_cols[h]
            qn = qkv_ref[3 * slot(j, h)] * (GDN_HEAD_DIM ** -0.5)
            kn = qkv_ref[3 * slot(j, h) + 1]
            v = qkv_ref[3 * slot(j, h) + 2]
            dcol = decay_cols[:, h:h + 1]
            drow = decay_rows[h:h + 1, :]
            lmask = jnp.where(causal, jnp.exp(jnp.where(causal, dcol - drow, 0.0)), 0.0)
            e_dec = jnp.exp(dcol)
            e_rem = jnp.exp(decay_last[:, h:h + 1] - dcol)
            beta = beta_all[:, GDN_HEADS + h:GDN_HEADS + h + 1]
            kbeta = kn * beta
            kn_b = kn.astype(BF16)
            a_ref[slot(j, h)] = jnp.where(strict, _dot_nt(kbeta.astype(BF16), kn_b) * lmask, 0.0)
            intra_ref[j, :, cols] = jnp.where(causal, _dot_nt(qn.astype(BF16), kn_b) * lmask, 0.0).astype(BF16)
            rhs_ref[slot(j, h)] = jnp.concatenate([v * beta, kbeta * e_dec], axis=1).astype(BF16)
            qd_ref[j, :, cols] = (qn * e_dec).astype(BF16)
            kdt_ref[j, :, cols] = (kn * e_rem).T.astype(BF16)


def _gdn(gqkv, gab, gz, conv_w, gate_params, norm_g, batch, seq):
    t = gqkv.shape[0]
    group = GDN_STEP_CHUNKS * GDN_CHUNK
    groups_per_seq = seq // group
    n_groups = batch * groups_per_seq
    per8 = group // SUBLANES
    n_slots = GDN_STEP_CHUNKS * GDN_HEADS
    cur = lambda s: jnp.minimum(s, n_groups - 1)
    done = lambda s: jnp.maximum(s - 1, 0)
    vmem = lambda shape, dt: pltpu.VMEM(shape, dt)
    return pl.pallas_call(
        functools.partial(_gdn_body, groups_per_seq=groups_per_seq, n_groups=n_groups),
        grid=(n_groups + 1,),
        in_specs=[_resident((SUBLANES, LANES)), _resident((CONV_K, 3 * GDN_W)),
                  _resident((1, GDN_HEAD_DIM)),
                  pl.BlockSpec((group, 3 * GDN_W), lambda s: (cur(s), 0)),
                  pl.BlockSpec((SUBLANES, 3 * GDN_W), lambda s: (jnp.maximum(cur(s) * per8 - 1, 0), 0)),
                  pl.BlockSpec((group, LANES), lambda s: (cur(s), 0)),
                  pl.BlockSpec((group, GDN_W), lambda s: (done(s), 0))],
        out_specs=pl.BlockSpec((group, GDN_W), lambda s: (done(s), 0)),
        out_shape=jax.ShapeDtypeStruct((t, GDN_W), BF16),
        scratch_shapes=[vmem((SUBLANES + group, 3 * GDN_W), F32),
                        vmem((n_slots, GDN_CHUNK, GDN_CHUNK), F32),
                        vmem((n_slots, GDN_CHUNK, 2 * GDN_HEAD_DIM), BF16),
                        vmem((GDN_STEP_CHUNKS, GDN_CHUNK, GDN_W), BF16),
                        vmem((GDN_STEP_CHUNKS, GDN_HEAD_DIM, GDN_HEADS * GDN_CHUNK), BF16),
                        vmem((GDN_STEP_CHUNKS, GDN_CHUNK, GDN_HEADS * GDN_CHUNK), BF16),
                        vmem((GDN_STEP_CHUNKS, SUBLANES, LANES), F32),
                        vmem((GDN_HEADS, GDN_HEAD_DIM, GDN_HEAD_DIM), F32),
                        vmem((3 * n_slots, GDN_CHUNK, GDN_HEAD_DIM), F32)],
        compiler_params=_cparams("arbitrary"),
        name="gdn",
    )(gate_params, conv_w, norm_g, gqkv, gqkv, gab, gz)


def _memkv_body(m_ref, w_ref, k_ref, v_ref):
    mb = m_ref[...].astype(BF16)
    k_ref[...] = _dot(mb, w_ref[:, 0:D_MODEL]).astype(BF16)
    v_ref[...] = _dot(mb, w_ref[:, D_MODEL:2 * D_MODEL]).astype(BF16)


def _memkv(mem, w_kv, layer):
    rows = mem.shape[0]
    tile = min(ROW_TILE, rows)
    row = pl.BlockSpec((tile, D_MODEL), lambda i: (i, 0))
    return pl.pallas_call(
        _memkv_body,
        grid=(rows // tile,),
        in_specs=[row, _layer_weight(layer, (D_MODEL, 2 * D_MODEL))],
        out_specs=[row, row],
        out_shape=[jax.ShapeDtypeStruct((rows, D_MODEL), BF16)] * 2,
        compiler_params=_cparams("parallel"),
        name="mem_kv_proj",
    )(mem, w_kv)


def _mix_xattn_body(att_ref, gdn_ref, x_ref, wmix_ref, wq_ref, k_ref, v_ref, wo_ref,
                    g0_ref, b0_ref, g1_ref, b1_ref, o_ref):
    n_sub = ROW_TILE // SUB_ROWS
    rows = [slice(r * SUB_ROWS, (r + 1) * SUB_ROWS) for r in range(n_sub)]
    head_cols = [slice(h * X_HEAD_DIM, (h + 1) * X_HEAD_DIM) for h in range(X_HEADS)]

    def mix(r):
        y = (_dot(att_ref[rows[r], :], wmix_ref[0:ATT_Q, :])
             + _dot(gdn_ref[rows[r], :], wmix_ref[ATT_Q:ATT_Q + GDN_W, :]))
        return _layer_norm(ALPHA * x_ref[rows[r], :] + y, g0_ref[...], b0_ref[...])

    def project_q(x1):
        return (_dot(x1.astype(BF16), wq_ref[...]) * X_SCALE).astype(BF16)

    def scores(q):
        return [_dot_nt(q[:, c], k_ref[:, c]) for c in head_cols]

    def softmax(s_heads):
        out = []
        for s in s_heads:
            p = jnp.exp(s - jnp.max(s, axis=-1, keepdims=True))
            out.append((p * (1.0 / jnp.sum(p, axis=-1, keepdims=True))).astype(BF16))
        return out

    def attend(p_heads):
        return jnp.concatenate([_dot(p, v_ref[:, c]).astype(BF16) for p, c in zip(p_heads, head_cols)], axis=1)

    def finish(r, x1, heads):
        y = _dot(heads, wo_ref[...])
        o_ref[rows[r], :] = _layer_norm(ALPHA * x1 + y, g1_ref[...], b1_ref[...])

    x1 = [mix(r) for r in range(n_sub)]
    s_prev = scores(project_q(x1[0]))
    for r in range(n_sub):
        q_next = project_q(x1[r + 1]) if r + 1 < n_sub else None
        heads = attend(softmax(s_prev))
        if q_next is not None:
            s_prev = scores(q_next)
        finish(r, x1[r], heads)


def _mix_xattn(att, gdn, x, w_mix, wq, k_mem, v_mem, wo, ln0, ln1, layer, batch, seq):
    t = x.shape[0]
    steps = seq // ROW_TILE
    mem_len = k_mem.shape[0] // batch
    row = lambda n: pl.BlockSpec((ROW_TILE, n), lambda bb, i: (bb * steps + i, 0))
    mem = pl.BlockSpec((mem_len, D_MODEL), lambda bb, i: (bb, 0))
    weight = _layer_weight(layer, (D_MODEL, D_MODEL))
    vec = _resident((1, D_MODEL))
    return pl.pallas_call(
        _mix_xattn_body,
        grid=(batch, steps),
        in_specs=[row(ATT_Q), row(GDN_W), row(D_MODEL), weight, weight, mem, mem, weight, vec, vec, vec, vec],
        out_specs=row(D_MODEL),
        out_shape=jax.ShapeDtypeStruct((t, D_MODEL), F32),
        compiler_params=_cparams("parallel", "parallel"),
        name="mix_xattn_ln",
    )(att, gdn, x, w_mix, wq, k_mem, v_mem, wo, *ln0, *ln1)


def _mlp_ln_body(x_ref, w1_ref, w2_ref, g_ref, b_ref, o_ref):
    x = x_ref[...]
    xb = x.astype(BF16)
    n_chunks = D_FF // FF_CHUNK

    def hidden(c):
        cols = slice(c * FF_CHUNK, (c + 1) * FF_CHUNK)
        return jnp.square(jnp.maximum(_dot(xb, w1_ref[:, cols]), 0.0)).astype(BF16), cols

    y = None
    for c in range(n_chunks - 1):
        hid, cols = hidden(c)
        part = _dot(hid, w2_ref[cols, :])
        y = part if y is None else y + part
    hid, cols = hidden(n_chunks - 1)
    for r in range(ROW_TILE // SUB_ROWS):
        rows = slice(r * SUB_ROWS, (r + 1) * SUB_ROWS)
        y_r = y[rows, :] + _dot(hid[rows, :], w2_ref[cols, :])
        o_ref[rows, :] = _layer_norm(ALPHA * x[rows, :] + y_r, g_ref[...], b_ref[...])


def _mlp_ln(x, w1, w2, g, b, layer):
    t = x.shape[0]
    row = pl.BlockSpec((ROW_TILE, D_MODEL), lambda i: (i, 0))
    return pl.pallas_call(
        _mlp_ln_body,
        grid=(t // ROW_TILE,),
        in_specs=[row, _layer_weight(layer, (D_MODEL, D_FF)), _layer_weight(layer, (D_FF, D_MODEL)),
                  _resident((1, D_MODEL)), _resident((1, D_MODEL))],
        out_specs=row,
        out_shape=jax.ShapeDtypeStruct((t, D_MODEL), F32),
        compiler_params=_cparams("parallel"),
        name="mlp_ln",
    )(x, w1, w2, g, b)


def _pack_w_in(w_in):
    w = w_in.astype(BF16)
    o = 0
    aq = w[..., o:o + ATT_Q]; o += ATT_Q
    ak = w[..., o:o + ATT_KV]; o += ATT_KV
    av = w[..., o:o + ATT_KV]; o += ATT_KV
    gqkv = w[..., o:o + 3 * GDN_W]; o += 3 * GDN_W
    ga = w[..., o:o + GDN_HEADS]; o += GDN_HEADS
    gb = w[..., o:o + GDN_HEADS]; o += GDN_HEADS
    gz = w[..., o:o + GDN_W]
    aq = jnp.concatenate([aq[..., h * ATT_HEAD_DIM:(h + 1) * ATT_HEAD_DIM] for h in ATT_HEAD_ORDER], axis=-1)
    pad = jnp.zeros(w.shape[:-1] + (LANES - 2 * GDN_HEADS,), BF16)
    return jnp.concatenate([aq, ak, av, gqkv, gz, ga, gb, pad], axis=-1)


def _pack_w_mix(w_mix_out):
    w = w_mix_out.astype(BF16)
    att = jnp.concatenate([w[:, h * ATT_HEAD_DIM:(h + 1) * ATT_HEAD_DIM, :] for h in ATT_HEAD_ORDER], axis=1)
    return jnp.concatenate([att, w[:, ATT_Q:, :]], axis=1)


def _pad_lanes(v):
    return jnp.pad(v, (0, LANES - v.shape[0]))


def kernel(x, mem, w_in, conv_w, attn_sinks, a_log, dt_bias, gdn_norm_g, w_mix_out,
           wq_mem, wk_mem, wv_mem, wo_mem, w_ff1, w_ff2, ln_g, ln_b):
    batch, seq, _ = x.shape
    xt = x.reshape(batch * seq, D_MODEL)
    memt = mem.reshape(batch * mem.shape[1], D_MODEL)
    w_all = _pack_w_in(w_in)
    w_mix, wq, wo = _pack_w_mix(w_mix_out), wq_mem.astype(BF16), wo_mem.astype(BF16)
    w_kv = jnp.concatenate([wk_mem.astype(BF16), wv_mem.astype(BF16)], axis=-1)
    w1, w2 = w_ff1.astype(BF16), w_ff2.astype(BF16)
    for l in range(DEPTH):
        gate_params = jnp.zeros((SUBLANES, LANES), F32)
        gate_params = gate_params.at[0].set(_pad_lanes(a_log[l])).at[1].set(_pad_lanes(dt_bias[l]))
        ln = lambda i: (ln_g[l, i].reshape(1, D_MODEL), ln_b[l, i].reshape(1, D_MODEL))

        aq, akv, gqkv, gz, gab = _inproj(xt, w_all, l)
        att = _swa(aq, akv, attn_sinks[l], batch, seq)
        gdn = _gdn(gqkv, gab, gz, conv_w[l], gate_params, gdn_norm_g[l].reshape(1, GDN_HEAD_DIM), batch, seq)
        k_mem, v_mem = _memkv(memt, w_kv, l)
        xt = _mix_xattn(att, gdn, xt, w_mix, wq, k_mem, v_mem, wo, ln(0), ln(1), l, batch, seq)
        xt = _mlp_ln(xt, w1, w2, *ln(2), l)
    return xt.reshape(batch, seq, D_MODEL)
```

```python
import functools

import jax
import jax.numpy as jnp
from jax import lax
from jax.experimental import pallas as pl
from jax.experimental.pallas import tpu as pltpu

F32 = jnp.float32
BF16 = jnp.bfloat16

D_MODEL = 1024
DEPTH = 2
ATT_HEADS = 8
ATT_KV_HEADS = 2
ATT_HEAD_DIM = 64
ATT_GROUP = ATT_HEADS // ATT_KV_HEADS
WINDOW = 128
ATT_Q = ATT_HEADS * ATT_HEAD_DIM
ATT_KV = ATT_KV_HEADS * ATT_HEAD_DIM
GDN_HEADS = 4
GDN_HEAD_DIM = 128
GDN_W = GDN_HEADS * GDN_HEAD_DIM
CONV_K = 4
X_HEADS = 4
X_HEAD_DIM = D_MODEL // X_HEADS
D_FF = 4 * D_MODEL
ALPHA = (2 * DEPTH) ** 0.25
LN_EPS = 1e-5
RMS_EPS = 1e-6
NEG_INF = -1e30
ATT_SCALE = ATT_HEAD_DIM ** -0.5
X_SCALE = X_HEAD_DIM ** -0.5
ALIBI_SLOPES = tuple(2.0 ** (-8.0 * (h + 1) / ATT_HEADS) for h in range(ATT_HEADS))
ATT_HEAD_ORDER = tuple(h for p in range(ATT_GROUP) for h in (p, p + ATT_GROUP))

LANES = 128
SUBLANES = 8
ROW_TILE = 512
SUB_ROWS = 256
ATT_ROWS = 512
GDN_CHUNK = 128
GDN_STEP_CHUNKS = 4
FF_CHUNK = 512
VMEM_LIMIT = 56 * 1024 * 1024

_C_AQ = (0, ATT_Q)
_C_AKV = (_C_AQ[1], _C_AQ[1] + 2 * LANES)
_C_GQKV = (_C_AKV[1], _C_AKV[1] + 3 * GDN_W)
_C_GZ = (_C_GQKV[1], _C_GQKV[1] + GDN_W)
_C_GAB = (_C_GZ[1], _C_GZ[1] + LANES)
N_IN = _C_GAB[1]


def _cparams(*sem):
    return pltpu.CompilerParams(dimension_semantics=sem, vmem_limit_bytes=VMEM_LIMIT)


def _resident(shape):
    nd = len(shape)
    return pl.BlockSpec(shape, lambda *_: (0,) * nd, pipeline_mode=pl.Buffered(1))


def _layer_weight(layer, shape):
    nd = len(shape)
    return pl.BlockSpec((None,) + tuple(shape), lambda *_: (layer,) + (0,) * nd, pipeline_mode=pl.Buffered(1))


def _dot(a, b):
    return jnp.dot(a, b, preferred_element_type=F32)


def _dot_nt(a, b):
    return lax.dot_general(a, b, (((1,), (1,)), ((), ())), preferred_element_type=F32)


def _layer_norm(xf, g, b):
    mu = jnp.mean(xf, axis=-1, keepdims=True)
    d = xf - mu
    var = jnp.mean(d * d, axis=-1, keepdims=True)
    return d * lax.rsqrt(var + LN_EPS) * g + b


def _silu(x):
    h = 0.5 * x
    return h + h * jnp.tanh(h)


def _inproj_body(x_ref, w_ref, aq_ref, akv_ref, gqkv_ref, gz_ref, gab_ref):
    xb = x_ref[...].astype(BF16)

    def mm(lo, hi):
        return _dot(xb, w_ref[:, lo:hi])

    aq_ref[...] = mm(*_C_AQ).astype(BF16)
    akv_ref[...] = mm(*_C_AKV).astype(BF16)
    for c in range(3):
        lo = _C_GQKV[0] + c * GDN_W
        gqkv_ref[:, c * GDN_W:(c + 1) * GDN_W] = mm(lo, lo + GDN_W)
    gz_ref[...] = mm(*_C_GZ)
    gab_ref[...] = mm(*_C_GAB)


def _inproj(x, w_all, layer):
    t = x.shape[0]
    row = lambda n: pl.BlockSpec((ROW_TILE, n), lambda i: (i, 0))
    return pl.pallas_call(
        _inproj_body,
        grid=(t // ROW_TILE,),
        in_specs=[row(D_MODEL), _layer_weight(layer, (D_MODEL, N_IN))],
        out_specs=[row(ATT_Q), row(2 * LANES), row(3 * GDN_W), row(GDN_W), row(LANES)],
        out_shape=[jax.ShapeDtypeStruct((t, ATT_Q), BF16),
                   jax.ShapeDtypeStruct((t, 2 * LANES), BF16),
                   jax.ShapeDtypeStruct((t, 3 * GDN_W), F32),
                   jax.ShapeDtypeStruct((t, GDN_W), F32),
                   jax.ShapeDtypeStruct((t, LANES), F32)],
        compiler_params=_cparams("parallel"),
        name="in_proj",
    )(x, w_all)


def _swa_body(sinks_ref, q_ref, kvc_ref, kvp_ref, o_ref):
    first = pl.program_id(1) == 0
    lane = lax.broadcasted_iota(jnp.int32, (1, LANES), 1)
    q_masks = (jnp.where(lane < ATT_HEAD_DIM, ATT_SCALE, 0.0).astype(BF16),
               jnp.where(lane >= ATT_HEAD_DIM, ATT_SCALE, 0.0).astype(BF16))
    low_half = lax.broadcasted_iota(jnp.int32, (WINDOW, LANES), 1) < ATT_HEAD_DIM
    qi = lax.broadcasted_iota(jnp.int32, (WINDOW, WINDOW), 0)
    kj = lax.broadcasted_iota(jnp.int32, (WINDOW, WINDOW), 1)
    in_cur = kj <= qi
    distf = jnp.where(in_cur, qi - kj, qi + WINDOW - kj).astype(F32)
    visible_first = in_cur | (kj >= jnp.where(first, WINDOW, 0))

    n_blk = ATT_ROWS // WINDOW
    rows = [slice(blk * WINDOW, (blk + 1) * WINDOW) for blk in range(n_blk)]

    def band(blk):
        prev = kvp_ref[...] if blk == 0 else kvc_ref[rows[blk - 1], :]
        return jnp.concatenate([prev, kvc_ref[rows[blk], :]], axis=0)

    def scores(blk):
        kk = band(blk)[:, 0:LANES]
        out = []
        for pos in range(ATT_HEADS):
            qp = q_ref[rows[blk], (pos // 2) * LANES:(pos // 2 + 1) * LANES]
            out.append(_dot_nt(qp * q_masks[pos % 2], kk))
        return out

    def attend(blk, s2_heads):
        vv = band(blk)[:, LANES:2 * LANES]
        halves = []
        for pos, s2 in enumerate(s2_heads):
            hq = ATT_HEAD_ORDER[pos]
            s = jnp.where(in_cur, s2[:, WINDOW:], s2[:, :WINDOW]) - ALIBI_SLOPES[hq] * distf
            if blk == 0:
                s = jnp.where(visible_first, s, NEG_INF)
            sink = sinks_ref[hq]
            m = jnp.maximum(jnp.max(s, axis=-1, keepdims=True), sink)
            p = jnp.exp(s - m)
            denom = jnp.sum(p, axis=-1, keepdims=True) + jnp.exp(sink - m)
            pn = p * (1.0 / denom)
            p2 = jnp.concatenate([jnp.where(in_cur, 0.0, pn), jnp.where(in_cur, pn, 0.0)], axis=1)
            halves.append(_dot(p2.astype(BF16), vv))
        for pair in range(ATT_HEADS // 2):
            o_ref[rows[blk], pair * LANES:(pair + 1) * LANES] = jnp.where(
                low_half, halves[2 * pair], halves[2 * pair + 1]).astype(BF16)

    s2_heads = scores(0)
    for blk in range(n_blk):
        s2_next = scores(blk + 1) if blk + 1 < n_blk else None
        attend(blk, s2_heads)
        s2_heads = s2_next


def _swa(aq, akv, sinks, batch, seq):
    t = aq.shape[0]
    steps = seq // ATT_ROWS
    blocks_per_step = ATT_ROWS // WINDOW
    blocks_per_seq = seq // WINDOW

    def prev_map(b, j):
        return (b * blocks_per_seq + jnp.maximum(j * blocks_per_step - 1, 0), 0)

    return pl.pallas_call(
        _swa_body,
        grid=(batch, steps),
        in_specs=[pl.BlockSpec(memory_space=pltpu.SMEM),
                  pl.BlockSpec((ATT_ROWS, ATT_Q), lambda b, j: (b * steps + j, 0)),
                  pl.BlockSpec((ATT_ROWS, 2 * LANES), lambda b, j: (b * steps + j, 0)),
                  pl.BlockSpec((WINDOW, 2 * LANES), prev_map)],
        out_specs=pl.BlockSpec((ATT_ROWS, ATT_Q), lambda b, j: (b * steps + j, 0)),
        out_shape=jax.ShapeDtypeStruct((t, ATT_Q), BF16),
        compiler_params=_cparams("parallel", "parallel"),
        name="swa_attention",
    )(sinks, aq, akv, akv)


def _unit_lower_inverse_stages(mats, ii, jj, out):
    base = 16

    def same_block(size):
        return (ii // size) == (jj // size)

    def bf(ms):
        return [m.astype(BF16) for m in ms]

    eye = (ii == jj).astype(F32)
    in_base = same_block(base)
    b1 = [jnp.where(in_base, -a, 0.0) for a in mats]
    b1_b = bf(b1)
    b2_b = bf([_dot(b, b) for b in b1_b])
    yield
    t = [eye + b for b in b1]
    b4 = [_dot(b, b) for b in b2_b]
    t = [x + _dot(xb, p) for x, xb, p in zip(t, bf(t), b2_b)]
    b4_b = bf(b4)
    yield
    b8 = [_dot(b, b) for b in b4_b]
    t = [x + _dot(xb, p) for x, xb, p in zip(t, bf(t), b4_b)]
    b8_b = bf(b8)
    yield
    t = [x + _dot(xb, p) for x, xb, p in zip(t, bf(t), b8_b)]
    yield
    size = base
    while size < GDN_CHUNK:
        off_diag = same_block(2 * size) & jnp.logical_not(same_block(size))
        t_b = bf(t)
        ct_b = bf([_dot(jnp.where(off_diag, a, 0.0).astype(BF16), xb) for a, xb in zip(mats, t_b)])
        yield
        t = [x - _dot(xb, y) for x, xb, y in zip(t, t_b, ct_b)]
        yield
        size *= 2
    out.extend(t)


def _gdn_body(gp_ref, convw_ref, normg_ref, x_ref, xprev_ref, gab_ref, z_ref, o_ref,
              xs_ref, a_ref, rhs_ref, qd_ref, kdt_ref, intra_ref, cd_ref, state_ref, qkv_ref,
              *, groups_per_seq, n_groups):
    step = pl.program_id(0)
    c_rows = GDN_CHUNK
    subs = range(GDN_STEP_CHUNKS)
    heads = range(GDN_HEADS)
    pairs = [(j, h) for j in subs for h in heads]
    slot = lambda j, h: j * GDN_HEADS + h
    head_cols = [slice(h * GDN_HEAD_DIM, (h + 1) * GDN_HEAD_DIM) for h in heads]
    chunk_rows = [slice(j * c_rows, (j + 1) * c_rows) for j in subs]

    @pl.when(step == 0)
    def _():
        for ref in (a_ref, rhs_ref, qd_ref, kdt_ref, intra_ref, cd_ref, state_ref):
            ref[...] = jnp.zeros_like(ref)

    ii = lax.broadcasted_iota(jnp.int32, (c_rows, c_rows), 0)
    jj = lax.broadcasted_iota(jnp.int32, (c_rows, c_rows), 1)
    causal = jj <= ii
    strict = jj < ii

    def finish():
        t_inv = []
        yield from _unit_lower_inverse_stages([a_ref[slot(j, h)] for j, h in pairs], ii, jj, t_inv)
        uw = [_dot(t.astype(BF16), rhs_ref[slot(j, h)]) for (j, h), t in zip(pairs, t_inv)]
        yield
        keep = jnp.where(lax.rem(step - 1, groups_per_seq) == 0, 0.0, 1.0)
        state = [state_ref[h] * keep for h in heads]
        for j in subs:
            r = [_dot(jnp.concatenate([uw[slot(j, h)][:, GDN_HEAD_DIM:].astype(BF16),
                                       qd_ref[j, :, head_cols[h]]], axis=0),
                      state[h].astype(BF16)) for h in heads]
            yield
            v_new_b = [(uw[slot(j, h)][:, :GDN_HEAD_DIM] - r[h][:c_rows]).astype(BF16) for h in heads]
            state = [state[h] * cd_ref[j, 0:1, h:h + 1] + _dot(kdt_ref[j, :, head_cols[h]], v_new_b[h])
                     for h in heads]
            yield
            for h in heads:
                o = r[h][c_rows:] + _dot(intra_ref[j, :, head_cols[h]], v_new_b[h])
                o = (o * lax.rsqrt(jnp.mean(o * o, axis=-1, keepdims=True) + RMS_EPS)
                     * normg_ref[...] * _silu(z_ref[chunk_rows[j], head_cols[h]]))
                o_ref[chunk_rows[j], head_cols[h]] = o.astype(BF16)
            yield
        for h in heads:
            state_ref[h] = state[h]

    first = lax.rem(jnp.minimum(step, n_groups - 1), groups_per_seq) == 0
    xs_ref[0:SUBLANES, :] = jnp.where(first, 0.0, xprev_ref[...])
    xs_ref[SUBLANES:SUBLANES + GDN_STEP_CHUNKS * c_rows, :] = x_ref[...]

    def convolve():
        for j, h in pairs:
            for part in range(3):
                col = part * GDN_W + h * GDN_HEAD_DIM
                window = xs_ref[j * c_rows:j * c_rows + SUBLANES + c_rows, col:col + LANES]
                w0, w1, w2, w3 = (convw_ref[tap:tap + 1, col:col + LANES] for tap in range(CONV_K))
                x1 = pltpu.roll(window, 1, axis=0)
                u = w1 * window + w0 * x1
                acc = (w3 * window + w2 * x1 + pltpu.roll(u, 2, axis=0))[SUBLANES:, :]
                y = _silu(acc)
                if part < 2:
                    y = y * lax.rsqrt(jnp.sum(y * y, axis=-1, keepdims=True) + RMS_EPS)
                qkv_ref[3 * slot(j, h) + part] = y
                yield

    finishing, convolving = finish(), convolve()
    n_finish, n_conv = 11 + 3 * GDN_STEP_CHUNKS, 3 * len(pairs)
    done = 0
    for turn, _ in enumerate(finishing):
        while done * n_finish < (turn + 1) * n_conv and next(convolving, True) is None:
            done += 1
    for _ in convolving:
        pass

    a_log = gp_ref[0:1, :]
    dt_bias = gp_ref[1:2, :]
    ltri = causal.astype(F32)
    for j in subs:
        gab = gab_ref[chunk_rows[j], :]
        g_all = -jnp.exp(a_log) * jax.nn.softplus(gab + dt_bias)
        beta_all = jax.nn.sigmoid(gab)
        decay_cols = jnp.dot(ltri, g_all, precision=lax.Precision.HIGHEST, preferred_element_type=F32)
        decay_rows = lax.dot_general(g_all.T[0:SUBLANES, :], ltri, (((1,), (1,)), ((), ())),
                                     precision=lax.Precision.HIGHEST, preferred_element_type=F32)
        decay_last = decay_cols[c_rows - 1:c_rows, :]
        cd_ref[j] = jnp.broadcast_to(jnp.exp(decay_last), (SUBLANES, LANES))
        for h in heads:
            cols = head_cols[h]
            qn = qkv_ref[3 * slot(j, h)] * (GDN_HEAD_DIM ** -0.5)
            kn = qkv_ref[3 * slot(j, h) + 1]
            v = qkv_ref[3 * slot(j, h) + 2]
            dcol = decay_cols[:, h:h + 1]
            drow = decay_rows[h:h + 1, :]
            lmask = jnp.where(causal, jnp.exp(jnp.where(causal, dcol - drow, 0.0)), 0.0)
            e_dec = jnp.exp(dcol)
            e_rem = jnp.exp(decay_last[:, h:h + 1] - dcol)
            beta = beta_all[:, GDN_HEADS + h:GDN_HEADS + h + 1]
            kbeta = kn * beta
            kn_b = kn.astype(BF16)
            a_ref[slot(j, h)] = _dot_nt(kbeta.astype(BF16), kn_b) * jnp.where(strict, lmask, 0.0)
            intra_ref[j, :, cols] = (_dot_nt(qn.astype(BF16), kn_b) * lmask).astype(BF16)
            rhs_ref[slot(j, h)] = jnp.concatenate([v * beta, kbeta * e_dec], axis=1).astype(BF16)
            qd_ref[j, :, cols] = (qn * e_dec).astype(BF16)
            kdt_ref[j, :, cols] = (kn * e_rem).T.astype(BF16)


def _gdn(gqkv, gab, gz, conv_w, gate_params, norm_g, batch, seq):
    t = gqkv.shape[0]
    group = GDN_STEP_CHUNKS * GDN_CHUNK
    groups_per_seq = seq // group
    n_groups = batch * groups_per_seq
    per8 = group // SUBLANES
    n_slots = GDN_STEP_CHUNKS * GDN_HEADS
    cur = lambda s: jnp.minimum(s, n_groups - 1)
    done = lambda s: jnp.maximum(s - 1, 0)
    vmem = lambda shape, dt: pltpu.VMEM(shape, dt)
    return pl.pallas_call(
        functools.partial(_gdn_body, groups_per_seq=groups_per_seq, n_groups=n_groups),
        grid=(n_groups + 1,),
        in_specs=[_resident((SUBLANES, LANES)), _resident((CONV_K, 3 * GDN_W)),
                  _resident((1, GDN_HEAD_DIM)),
                  pl.BlockSpec((group, 3 * GDN_W), lambda s: (cur(s), 0)),
                  pl.BlockSpec((SUBLANES, 3 * GDN_W), lambda s: (jnp.maximum(cur(s) * per8 - 1, 0), 0)),
                  pl.BlockSpec((group, LANES), lambda s: (cur(s), 0)),
                  pl.BlockSpec((group, GDN_W), lambda s: (done(s), 0))],
        out_specs=pl.BlockSpec((group, GDN_W), lambda s: (done(s), 0)),
        out_shape=jax.ShapeDtypeStruct((t, GDN_W), BF16),
        scratch_shapes=[vmem((SUBLANES + group, 3 * GDN_W), F32),
                        vmem((n_slots, GDN_CHUNK, GDN_CHUNK), F32),
                        vmem((n_slots, GDN_CHUNK, 2 * GDN_HEAD_DIM), BF16),
                        vmem((GDN_STEP_CHUNKS, GDN_CHUNK, GDN_W), BF16),
                        vmem((GDN_STEP_CHUNKS, GDN_HEAD_DIM, GDN_HEADS * GDN_CHUNK), BF16),
                        vmem((GDN_STEP_CHUNKS, GDN_CHUNK, GDN_HEADS * GDN_CHUNK), BF16),
                        vmem((GDN_STEP_CHUNKS, SUBLANES, LANES), F32),
                        vmem((GDN_HEADS, GDN_HEAD_DIM, GDN_HEAD_DIM), F32),
                        vmem((3 * n_slots, GDN_CHUNK, GDN_HEAD_DIM), F32)],
        compiler_params=_cparams("arbitrary"),
        name="gdn",
    )(gate_params, conv_w, norm_g, gqkv, gqkv, gab, gz)


def _memkv_body(m_ref, w_ref, k_ref, v_ref):
    mb = m_ref[...].astype(BF16)
    k_ref[...] = _dot(mb, w_ref[:, 0:D_MODEL]).astype(BF16)
    v_ref[...] = _dot(mb, w_ref[:, D_MODEL:2 * D_MODEL]).astype(BF16)


def _memkv(mem, w_kv, layer):
    rows = mem.shape[0]
    tile = min(ROW_TILE, rows)
    row = pl.BlockSpec((tile, D_MODEL), lambda i: (i, 0))
    return pl.pallas_call(
        _memkv_body,
        grid=(rows // tile,),
        in_specs=[row, _layer_weight(layer, (D_MODEL, 2 * D_MODEL))],
        out_specs=[row, row],
        out_shape=[jax.ShapeDtypeStruct((rows, D_MODEL), BF16)] * 2,
        compiler_params=_cparams("parallel"),
        name="mem_kv_proj",
    )(mem, w_kv)


def _mix_xattn_body(att_ref, gdn_ref, x_ref, wmix_ref, wq_ref, k_ref, v_ref, wo_ref,
                    g0_ref, b0_ref, g1_ref, b1_ref, o_ref):
    n_sub = ROW_TILE // SUB_ROWS
    rows = [slice(r * SUB_ROWS, (r + 1) * SUB_ROWS) for r in range(n_sub)]
    head_cols = [slice(h * X_HEAD_DIM, (h + 1) * X_HEAD_DIM) for h in range(X_HEADS)]

    def mix(r):
        y = (_dot(att_ref[rows[r], :], wmix_ref[0:ATT_Q, :])
             + _dot(gdn_ref[rows[r], :], wmix_ref[ATT_Q:ATT_Q + GDN_W, :]))
        return _layer_norm(ALPHA * x_ref[rows[r], :] + y, g0_ref[...], b0_ref[...])

    def project_q(x1):
        return (_dot(x1.astype(BF16), wq_ref[...]) * X_SCALE).astype(BF16)

    def scores(q):
        return [_dot_nt(q[:, c], k_ref[:, c]) for c in head_cols]

    def softmax(s_heads):
        out = []
        for s in s_heads:
            p = jnp.exp(s - jnp.max(s, axis=-1, keepdims=True))
            out.append((p * (1.0 / jnp.sum(p, axis=-1, keepdims=True))).astype(BF16))
        return out

    def attend(p_heads):
        return jnp.concatenate([_dot(p, v_ref[:, c]).astype(BF16) for p, c in zip(p_heads, head_cols)], axis=1)

    def finish(r, x1, heads):
        y = _dot(heads, wo_ref[...])
        o_ref[rows[r], :] = _layer_norm(ALPHA * x1 + y, g1_ref[...], b1_ref[...])

    x1 = [mix(r) for r in range(n_sub)]
    s_prev = scores(project_q(x1[0]))
    for r in range(n_sub):
        q_next = project_q(x1[r + 1]) if r + 1 < n_sub else None
        heads = attend(softmax(s_prev))
        if q_next is not None:
            s_prev = scores(q_next)
        finish(r, x1[r], heads)


def _mix_xattn(att, gdn, x, w_mix, wq, k_mem, v_mem, wo, ln0, ln1, layer, batch, seq):
    t = x.shape[0]
    steps = seq // ROW_TILE
    mem_len = k_mem.shape[0] // batch
    row = lambda n: pl.BlockSpec((ROW_TILE, n), lambda bb, i: (bb * steps + i, 0))
    mem = pl.BlockSpec((mem_len, D_MODEL), lambda bb, i: (bb, 0))
    weight = _layer_weight(layer, (D_MODEL, D_MODEL))
    vec = _resident((1, D_MODEL))
    return pl.pallas_call(
        _mix_xattn_body,
        grid=(batch, steps),
        in_specs=[row(ATT_Q), row(GDN_W), row(D_MODEL), weight, weight, mem, mem, weight, vec, vec, vec, vec],
        out_specs=row(D_MODEL),
        out_shape=jax.ShapeDtypeStruct((t, D_MODEL), F32),
        compiler_params=_cparams("parallel", "parallel"),
        name="mix_xattn_ln",
    )(att, gdn, x, w_mix, wq, k_mem, v_mem, wo, *ln0, *ln1)


def _mlp_ln_body(x_ref, w1_ref, w2_ref, g_ref, b_ref, o_ref):
    x = x_ref[...]
    xb = x.astype(BF16)
    n_chunks = D_FF // FF_CHUNK

    def hidden(c):
        cols = slice(c * FF_CHUNK, (c + 1) * FF_CHUNK)
        return jnp.square(jnp.maximum(_dot(xb, w1_ref[:, cols]), 0.0)).astype(BF16), cols

    y = None
    for c in range(n_chunks - 1):
        hid, cols = hidden(c)
        part = _dot(hid, w2_ref[cols, :])
        y = part if y is None else y + part
    hid, cols = hidden(n_chunks - 1)
    for r in range(ROW_TILE // SUB_ROWS):
        rows = slice(r * SUB_ROWS, (r + 1) * SUB_ROWS)
        y_r = y[rows, :] + _dot(hid[rows, :], w2_ref[cols, :])
        o_ref[rows, :] = _layer_norm(ALPHA * x[rows, :] + y_r, g_ref[...], b_ref[...])


def _mlp_ln(x, w1, w2, g, b, layer):
    t = x.shape[0]
    row = pl.BlockSpec((ROW_TILE, D_MODEL), lambda i: (i, 0))
    return pl.pallas_call(
        _mlp_ln_body,
        grid=(t // ROW_TILE,),
        in_specs=[row, _layer_weight(layer, (D_MODEL, D_FF)), _layer_weight(layer, (D_FF, D_MODEL)),
                  _resident((1, D_MODEL)), _resident((1, D_MODEL))],
        out_specs=row,
        out_shape=jax.ShapeDtypeStruct((t, D_MODEL), F32),
        compiler_params=_cparams("parallel"),
        name="mlp_ln",
    )(x, w1, w2, g, b)


def _pack_w_in(w_in):
    w = w_in.astype(BF16)
    o = 0
    aq = w[..., o:o + ATT_Q]; o += ATT_Q
    ak = w[..., o:o + ATT_KV]; o += ATT_KV
    av = w[..., o:o + ATT_KV]; o += ATT_KV
    gqkv = w[..., o:o + 3 * GDN_W]; o += 3 * GDN_W
    ga = w[..., o:o + GDN_HEADS]; o += GDN_HEADS
    gb = w[..., o:o + GDN_HEADS]; o += GDN_HEADS
    gz = w[..., o:o + GDN_W]
    aq = jnp.concatenate([aq[..., h * ATT_HEAD_DIM:(h + 1) * ATT_HEAD_DIM] for h in ATT_HEAD_ORDER], axis=-1)
    pad = jnp.zeros(w.shape[:-1] + (LANES - 2 * GDN_HEADS,), BF16)
    return jnp.concatenate([aq, ak, av, gqkv, gz, ga, gb, pad], axis=-1)


def _pack_w_mix(w_mix_out):
    w = w_mix_out.astype(BF16)
    att = jnp.concatenate([w[:, h * ATT_HEAD_DIM:(h + 1) * ATT_HEAD_DIM, :] for h in ATT_HEAD_ORDER], axis=1)
    return jnp.concatenate([att, w[:, ATT_Q:, :]], axis=1)


def _pad_lanes(v):
    return jnp.pad(v, (0, LANES - v.shape[0]))


def kernel(x, mem, w_in, conv_w, attn_sinks, a_log, dt_bias, gdn_norm_g, w_mix_out,
           wq_mem, wk_mem, wv_mem, wo_mem, w_ff1, w_ff2, ln_g, ln_b):
    batch, seq, d_model = x.shape
    assert d_model == D_MODEL and mem.shape[0] == batch and mem.shape[2] == D_MODEL
    assert w_in.shape == (DEPTH, D_MODEL, ATT_Q + 2 * ATT_KV + 4 * GDN_W + 2 * GDN_HEADS)
    assert seq % ROW_TILE == 0 and seq % ATT_ROWS == 0 and seq % (GDN_STEP_CHUNKS * GDN_CHUNK) == 0
    assert (batch * mem.shape[1]) % min(ROW_TILE, batch * mem.shape[1]) == 0
    xt = x.reshape(batch * seq, D_MODEL)
    memt = mem.reshape(batch * mem.shape[1], D_MODEL)
    w_all = _pack_w_in(w_in)
    w_mix, wq, wo = _pack_w_mix(w_mix_out), wq_mem.astype(BF16), wo_mem.astype(BF16)
    w_kv = jnp.concatenate([wk_mem.astype(BF16), wv_mem.astype(BF16)], axis=-1)
    w1, w2 = w_ff1.astype(BF16), w_ff2.astype(BF16)
    for l in range(DEPTH):
        gate_params = jnp.zeros((SUBLANES, LANES), F32)
        gate_params = gate_params.at[0].set(_pad_lanes(a_log[l])).at[1].set(_pad_lanes(dt_bias[l]))
        ln = lambda i: (ln_g[l, i].reshape(1, D_MODEL), ln_b[l, i].reshape(1, D_MODEL))

        aq, akv, gqkv, gz, gab = _inproj(xt, w_all, l)
        att = _swa(aq, akv, attn_sinks[l], batch, seq)
        gdn = _gdn(gqkv, gab, gz, conv_w[l], gate_params, gdn_norm_g[l].reshape(1, GDN_HEAD_DIM), batch, seq)
        k_mem, v_mem = _memkv(memt, w_kv, l)
        xt = _mix_xattn(att, gdn, xt, w_mix, wq, k_mem, v_mem, wo, ln(0), ln(1), l, batch, seq)
        xt = _mlp_ln(xt, w1, w2, *ln(2), l)
    return xt.reshape(batch, seq, D_MODEL)
```

```python
import functools

import jax
import jax.numpy as jnp
from jax import lax
from jax.experimental import pallas as pl
from jax.experimental.pallas import tpu as pltpu

F32 = jnp.float32
BF16 = jnp.bfloat16

D_MODEL = 1024
DEPTH = 2
ATT_HEADS = 8
ATT_KV_HEADS = 2
ATT_HEAD_DIM = 64
ATT_GROUP = ATT_HEADS // ATT_KV_HEADS
WINDOW = 128
ATT_Q = ATT_HEADS * ATT_HEAD_DIM
ATT_KV = ATT_KV_HEADS * ATT_HEAD_DIM
GDN_HEADS = 4
GDN_HEAD_DIM = 128
GDN_W = GDN_HEADS * GDN_HEAD_DIM
CONV_K = 4
X_HEADS = 4
X_HEAD_DIM = D_MODEL // X_HEADS
D_FF = 4 * D_MODEL
ALPHA = (2 * DEPTH) ** 0.25
LN_EPS = 1e-5
RMS_EPS = 1e-6
NEG_INF = -1e30
ATT_SCALE = ATT_HEAD_DIM ** -0.5
X_SCALE = X_HEAD_DIM ** -0.5
ALIBI_SLOPES = tuple(2.0 ** (-8.0 * (h + 1) / ATT_HEADS) for h in range(ATT_HEADS))
ATT_HEAD_ORDER = tuple(h for p in range(ATT_GROUP) for h in (p, p + ATT_GROUP))

LANES = 128
SUBLANES = 8
ROW_TILE = 512
IN_ROWS = 1024
SUB_ROWS = 256
ATT_ROWS = 512
GDN_CHUNK = 128
GDN_STEP_CHUNKS = 4
FF_CHUNK = 2048
VMEM_LIMIT = 56 * 1024 * 1024

_C_AQ = (0, ATT_Q)
_C_AKV = (_C_AQ[1], _C_AQ[1] + 2 * LANES)
_C_GQKV = (_C_AKV[1], _C_AKV[1] + 3 * GDN_W)
_C_GZ = (_C_GQKV[1], _C_GQKV[1] + GDN_W)
_C_GAB = (_C_GZ[1], _C_GZ[1] + LANES)
N_IN = _C_GAB[1]


def _cparams(*sem):
    return pltpu.CompilerParams(dimension_semantics=sem, vmem_limit_bytes=VMEM_LIMIT)


def _resident(shape):
    nd = len(shape)
    return pl.BlockSpec(shape, lambda *_: (0,) * nd, pipeline_mode=pl.Buffered(1))


def _layer_weight(layer, shape):
    nd = len(shape)
    return pl.BlockSpec((None,) + tuple(shape), lambda *_: (layer,) + (0,) * nd, pipeline_mode=pl.Buffered(1))


def _dot(a, b):
    return jnp.dot(a, b, preferred_element_type=F32)


def _dot_nt(a, b):
    return lax.dot_general(a, b, (((1,), (1,)), ((), ())), preferred_element_type=F32)


def _layer_norm(xf, g, b):
    mu = jnp.mean(xf, axis=-1, keepdims=True)
    d = xf - mu
    var = jnp.mean(d * d, axis=-1, keepdims=True)
    return d * lax.rsqrt(var + LN_EPS) * g + b


def _silu(x):
    h = 0.5 * x
    return h + h * jnp.tanh(h)


def _inproj_body(x_ref, w_ref, aq_ref, akv_ref, gqkv_ref, gz_ref, gab_ref):
    xb = x_ref[...].astype(BF16)

    def mm(lo, hi):
        return _dot(xb, w_ref[:, lo:hi])

    aq_ref[...] = mm(*_C_AQ).astype(BF16)
    akv_ref[...] = mm(*_C_AKV).astype(BF16)
    for c in range(3):
        lo = _C_GQKV[0] + c * GDN_W
        gqkv_ref[:, c * GDN_W:(c + 1) * GDN_W] = mm(lo, lo + GDN_W)
    gz_ref[...] = mm(*_C_GZ)
    gab_ref[...] = mm(*_C_GAB)


def _inproj(x, w_all, layer):
    t = x.shape[0]
    row = lambda n: pl.BlockSpec((IN_ROWS, n), lambda i: (i, 0))
    return pl.pallas_call(
        _inproj_body,
        grid=(t // IN_ROWS,),
        in_specs=[row(D_MODEL), _layer_weight(layer, (D_MODEL, N_IN))],
        out_specs=[row(ATT_Q), row(2 * LANES), row(3 * GDN_W), row(GDN_W), row(LANES)],
        out_shape=[jax.ShapeDtypeStruct((t, ATT_Q), BF16),
                   jax.ShapeDtypeStruct((t, 2 * LANES), BF16),
                   jax.ShapeDtypeStruct((t, 3 * GDN_W), F32),
                   jax.ShapeDtypeStruct((t, GDN_W), F32),
                   jax.ShapeDtypeStruct((t, LANES), F32)],
        compiler_params=_cparams("parallel"),
        name="in_proj",
    )(x, w_all)


def _swa_body(sinks_ref, q_ref, kvc_ref, kvp_ref, o_ref):
    first = pl.program_id(1) == 0
    lane = lax.broadcasted_iota(jnp.int32, (1, LANES), 1)
    q_masks = (jnp.where(lane < ATT_HEAD_DIM, ATT_SCALE, 0.0).astype(BF16),
               jnp.where(lane >= ATT_HEAD_DIM, ATT_SCALE, 0.0).astype(BF16))
    low_half = lax.broadcasted_iota(jnp.int32, (WINDOW, LANES), 1) < ATT_HEAD_DIM
    qi = lax.broadcasted_iota(jnp.int32, (WINDOW, WINDOW), 0)
    kj = lax.broadcasted_iota(jnp.int32, (WINDOW, WINDOW), 1)
    in_cur = kj <= qi
    distf = jnp.where(in_cur, qi - kj, qi + WINDOW - kj).astype(F32)
    visible_first = in_cur | (kj >= jnp.where(first, WINDOW, 0))

    n_blk = ATT_ROWS // WINDOW
    rows = [slice(blk * WINDOW, (blk + 1) * WINDOW) for blk in range(n_blk)]

    def band(blk):
        prev = kvp_ref[...] if blk == 0 else kvc_ref[rows[blk - 1], :]
        return jnp.concatenate([prev, kvc_ref[rows[blk], :]], axis=0)

    def scores(blk):
        kk = band(blk)[:, 0:LANES]
        out = []
        for pos in range(ATT_HEADS):
            qp = q_ref[rows[blk], (pos // 2) * LANES:(pos // 2 + 1) * LANES]
            out.append(_dot_nt(qp * q_masks[pos % 2], kk))
        return out

    def attend(blk, s2_heads):
        vv = band(blk)[:, LANES:2 * LANES]
        halves = []
        for pos, s2 in enumerate(s2_heads):
            hq = ATT_HEAD_ORDER[pos]
            s = jnp.where(in_cur, s2[:, WINDOW:], s2[:, :WINDOW]) - ALIBI_SLOPES[hq] * distf
            if blk == 0:
                s = jnp.where(visible_first, s, NEG_INF)
            sink = sinks_ref[hq]
            m = jnp.maximum(jnp.max(s, axis=-1, keepdims=True), sink)
            p = jnp.exp(s - m)
            denom = jnp.sum(p, axis=-1, keepdims=True) + jnp.exp(sink - m)
            pn = p * (1.0 / denom)
            p2 = jnp.concatenate([jnp.where(in_cur, 0.0, pn), jnp.where(in_cur, pn, 0.0)], axis=1)
            halves.append(_dot(p2.astype(BF16), vv))
        for pair in range(ATT_HEADS // 2):
            o_ref[rows[blk], pair * LANES:(pair + 1) * LANES] = jnp.where(
                low_half, halves[2 * pair], halves[2 * pair + 1]).astype(BF16)

    s2_heads = scores(0)
    for blk in range(n_blk):
        s2_next = scores(blk + 1) if blk + 1 < n_blk else None
        attend(blk, s2_heads)
        s2_heads = s2_next


def _swa(aq, akv, sinks, batch, seq):
    t = aq.shape[0]
    steps = seq // ATT_ROWS
    blocks_per_step = ATT_ROWS // WINDOW
    blocks_per_seq = seq // WINDOW

    def prev_map(b, j):
        return (b * blocks_per_seq + jnp.maximum(j * blocks_per_step - 1, 0), 0)

    return pl.pallas_call(
        _swa_body,
        grid=(batch, steps),
        in_specs=[pl.BlockSpec(memory_space=pltpu.SMEM),
                  pl.BlockSpec((ATT_ROWS, ATT_Q), lambda b, j: (b * steps + j, 0)),
                  pl.BlockSpec((ATT_ROWS, 2 * LANES), lambda b, j: (b * steps + j, 0)),
                  pl.BlockSpec((WINDOW, 2 * LANES), prev_map)],
        out_specs=pl.BlockSpec((ATT_ROWS, ATT_Q), lambda b, j: (b * steps + j, 0)),
        out_shape=jax.ShapeDtypeStruct((t, ATT_Q), BF16),
        compiler_params=_cparams("parallel", "parallel"),
        name="swa_attention",
    )(sinks, aq, akv, akv)


def _unit_lower_inverse_stages(mats, ii, jj, out):
    base = 16

    def same_block(size):
        return (ii // size) == (jj // size)

    def bf(ms):
        return [m.astype(BF16) for m in ms]

    eye = (ii == jj).astype(F32)
    in_base = same_block(base)
    b1 = [jnp.where(in_base, -a, 0.0) for a in mats]
    b1_b = bf(b1)
    b2_b = bf([_dot(b, b) for b in b1_b])
    yield
    t = [eye + b for b in b1]
    b4 = [_dot(b, b) for b in b2_b]
    t = [x + _dot(xb, p) for x, xb, p in zip(t, bf(t), b2_b)]
    b4_b = bf(b4)
    yield
    b8 = [_dot(b, b) for b in b4_b]
    t = [x + _dot(xb, p) for x, xb, p in zip(t, bf(t), b4_b)]
    b8_b = bf(b8)
    yield
    t = [x + _dot(xb, p) for x, xb, p in zip(t, bf(t), b8_b)]
    yield
    size = base
    while size < GDN_CHUNK:
        off_diag = same_block(2 * size) & jnp.logical_not(same_block(size))
        t_b = bf(t)
        ct_b = bf([_dot(jnp.where(off_diag, a, 0.0).astype(BF16), xb) for a, xb in zip(mats, t_b)])
        yield
        t = [x - _dot(xb, y) for x, xb, y in zip(t, t_b, ct_b)]
        yield
        size *= 2
    out.extend(t)


def _gdn_body(gp_ref, convw_ref, normg_ref, x_ref, xprev_ref, gab_ref, z_ref, o_ref,
              xs_ref, a_ref, rhs_ref, qd_ref, kdt_ref, intra_ref, cd_ref, state_ref, qkv_ref,
              *, groups_per_seq, n_groups):
    step = pl.program_id(0)
    c_rows = GDN_CHUNK
    subs = range(GDN_STEP_CHUNKS)
    heads = range(GDN_HEADS)
    pairs = [(j, h) for j in subs for h in heads]
    slot = lambda j, h: j * GDN_HEADS + h
    head_cols = [slice(h * GDN_HEAD_DIM, (h + 1) * GDN_HEAD_DIM) for h in heads]
    chunk_rows = [slice(j * c_rows, (j + 1) * c_rows) for j in subs]

    @pl.when(step == 0)
    def _():
        for ref in (a_ref, rhs_ref, qd_ref, kdt_ref, intra_ref, cd_ref, state_ref):
            ref[...] = jnp.zeros_like(ref)

    ii = lax.broadcasted_iota(jnp.int32, (c_rows, c_rows), 0)
    jj = lax.broadcasted_iota(jnp.int32, (c_rows, c_rows), 1)
    causal = jj <= ii
    strict = jj < ii

    def finish():
        t_inv = []
        yield from _unit_lower_inverse_stages([a_ref[slot(j, h)] for j, h in pairs], ii, jj, t_inv)
        uw = [_dot(t.astype(BF16), rhs_ref[slot(j, h)]) for (j, h), t in zip(pairs, t_inv)]
        yield
        keep = jnp.where(lax.rem(step - 1, groups_per_seq) == 0, 0.0, 1.0)
        state = [state_ref[h] * keep for h in heads]
        for j in subs:
            r = [_dot(jnp.concatenate([uw[slot(j, h)][:, GDN_HEAD_DIM:].astype(BF16),
                                       qd_ref[j, :, head_cols[h]]], axis=0),
                      state[h].astype(BF16)) for h in heads]
            yield
            v_new_b = [(uw[slot(j, h)][:, :GDN_HEAD_DIM] - r[h][:c_rows]).astype(BF16) for h in heads]
            state = [state[h] * cd_ref[j, 0:1, h:h + 1] + _dot(kdt_ref[j, :, head_cols[h]], v_new_b[h])
                     for h in heads]
            yield
            for h in heads:
                o = r[h][c_rows:] + _dot(intra_ref[j, :, head_cols[h]], v_new_b[h])
                o = (o * lax.rsqrt(jnp.mean(o * o, axis=-1, keepdims=True) + RMS_EPS)
                     * normg_ref[...] * _silu(z_ref[chunk_rows[j], head_cols[h]]))
                o_ref[chunk_rows[j], head_cols[h]] = o.astype(BF16)
            yield
        for h in heads:
            state_ref[h] = state[h]

    first = lax.rem(jnp.minimum(step, n_groups - 1), groups_per_seq) == 0
    xs_ref[0:SUBLANES, :] = jnp.where(first, 0.0, xprev_ref[...])
    xs_ref[SUBLANES:SUBLANES + GDN_STEP_CHUNKS * c_rows, :] = x_ref[...]

    def convolve():
        for j, h in pairs:
            for part in range(3):
                col = part * GDN_W + h * GDN_HEAD_DIM
                window = xs_ref[j * c_rows:j * c_rows + SUBLANES + c_rows, col:col + LANES]
                w0, w1, w2, w3 = (convw_ref[tap:tap + 1, col:col + LANES] for tap in range(CONV_K))
                x1 = pltpu.roll(window, 1, axis=0)
                u = w1 * window + w0 * x1
                acc = (w3 * window + w2 * x1 + pltpu.roll(u, 2, axis=0))[SUBLANES:, :]
                y = _silu(acc)
                if part < 2:
                    y = y * lax.rsqrt(jnp.sum(y * y, axis=-1, keepdims=True) + RMS_EPS)
                qkv_ref[3 * slot(j, h) + part] = y
                yield

    finishing, convolving = finish(), convolve()
    n_finish, n_conv = 11 + 3 * GDN_STEP_CHUNKS, 3 * len(pairs)
    done = 0
    for turn, _ in enumerate(finishing):
        while done * n_finish < (turn + 1) * n_conv and next(convolving, True) is None:
            done += 1
    for _ in convolving:
        pass

    a_log = gp_ref[0:1, :]
    dt_bias = gp_ref[1:2, :]
    ltri = causal.astype(F32)
    for j in subs:
        gab = gab_ref[chunk_rows[j], :]
        g_all = -jnp.exp(a_log) * jax.nn.softplus(gab + dt_bias)
        beta_all = jax.nn.sigmoid(gab)
        decay_cols = jnp.dot(ltri, g_all, precision=lax.Precision.HIGHEST, preferred_element_type=F32)
        decay_rows = lax.dot_general(g_all.T[0:SUBLANES, :], ltri, (((1,), (1,)), ((), ())),
                                     precision=lax.Precision.HIGHEST, preferred_element_type=F32)
        decay_last = decay_cols[c_rows - 1:c_rows, :]
        cd_ref[j] = jnp.broadcast_to(jnp.exp(decay_last), (SUBLANES, LANES))
        for h in heads:
            cols = head_cols[h]
            qn = qkv_ref[3 * slot(j, h)] * (GDN_HEAD_DIM ** -0.5)
            kn = qkv_ref[3 * slot(j, h) + 1]
            v = qkv_ref[3 * slot(j, h) + 2]
            dcol = decay_cols[:, h:h + 1]
            drow = decay_rows[h:h + 1, :]
            lmask = jnp.where(causal, jnp.exp(jnp.where(causal, dcol - drow, 0.0)), 0.0)
            e_dec = jnp.exp(dcol)
            e_rem = jnp.exp(decay_last[:, h:h + 1] - dcol)
            beta = beta_all[:, GDN_HEADS + h:GDN_HEADS + h + 1]
            kbeta = kn * beta
            kn_b = kn.astype(BF16)
            a_ref[slot(j, h)] = _dot_nt(kbeta.astype(BF16), kn_b) * jnp.where(strict, lmask, 0.0)
            intra_ref[j, :, cols] = (_dot_nt(qn.astype(BF16), kn_b) * lmask).astype(BF16)
            rhs_ref[slot(j, h)] = jnp.concatenate([v * beta, kbeta * e_dec], axis=1).astype(BF16)
            qd_ref[j, :, cols] = (qn * e_dec).astype(BF16)
            kdt_ref[j, :, cols] = (kn * e_rem).T.astype(BF16)


def _gdn(gqkv, gab, gz, conv_w, gate_params, norm_g, batch, seq):
    t = gqkv.shape[0]
    group = GDN_STEP_CHUNKS * GDN_CHUNK
    groups_per_seq = seq // group
    n_groups = batch * groups_per_seq
    per8 = group // SUBLANES
    n_slots = GDN_STEP_CHUNKS * GDN_HEADS
    cur = lambda s: jnp.minimum(s, n_groups - 1)
    done = lambda s: jnp.maximum(s - 1, 0)
    vmem = lambda shape, dt: pltpu.VMEM(shape, dt)
    return pl.pallas_call(
        functools.partial(_gdn_body, groups_per_seq=groups_per_seq, n_groups=n_groups),
        grid=(n_groups + 1,),
        in_specs=[_resident((SUBLANES, LANES)), _resident((CONV_K, 3 * GDN_W)),
                  _resident((1, GDN_HEAD_DIM)),
                  pl.BlockSpec((group, 3 * GDN_W), lambda s: (cur(s), 0)),
                  pl.BlockSpec((SUBLANES, 3 * GDN_W), lambda s: (jnp.maximum(cur(s) * per8 - 1, 0), 0)),
                  pl.BlockSpec((group, LANES), lambda s: (cur(s), 0)),
                  pl.BlockSpec((group, GDN_W), lambda s: (done(s), 0))],
        out_specs=pl.BlockSpec((group, GDN_W), lambda s: (done(s), 0)),
        out_shape=jax.ShapeDtypeStruct((t, GDN_W), BF16),
        scratch_shapes=[vmem((SUBLANES + group, 3 * GDN_W), F32),
                        vmem((n_slots, GDN_CHUNK, GDN_CHUNK), F32),
                        vmem((n_slots, GDN_CHUNK, 2 * GDN_HEAD_DIM), BF16),
                        vmem((GDN_STEP_CHUNKS, GDN_CHUNK, GDN_W), BF16),
                        vmem((GDN_STEP_CHUNKS, GDN_HEAD_DIM, GDN_HEADS * GDN_CHUNK), BF16),
                        vmem((GDN_STEP_CHUNKS, GDN_CHUNK, GDN_HEADS * GDN_CHUNK), BF16),
                        vmem((GDN_STEP_CHUNKS, SUBLANES, LANES), F32),
                        vmem((GDN_HEADS, GDN_HEAD_DIM, GDN_HEAD_DIM), F32),
                        vmem((3 * n_slots, GDN_CHUNK, GDN_HEAD_DIM), F32)],
        compiler_params=_cparams("arbitrary"),
        name="gdn",
    )(gate_params, conv_w, norm_g, gqkv, gqkv, gab, gz)


def _memkv_body(m_ref, w_ref, k_ref, v_ref):
    mb = m_ref[...].astype(BF16)
    k_ref[...] = _dot(mb, w_ref[:, 0:D_MODEL]).astype(BF16)
    v_ref[...] = _dot(mb, w_ref[:, D_MODEL:2 * D_MODEL]).astype(BF16)


def _memkv(mem, w_kv, layer):
    rows = mem.shape[0]
    tile = min(ROW_TILE, rows)
    row = pl.BlockSpec((tile, D_MODEL), lambda i: (i, 0))
    return pl.pallas_call(
        _memkv_body,
        grid=(rows // tile,),
        in_specs=[row, _layer_weight(layer, (D_MODEL, 2 * D_MODEL))],
        out_specs=[row, row],
        out_shape=[jax.ShapeDtypeStruct((rows, D_MODEL), BF16)] * 2,
        compiler_params=_cparams("parallel"),
        name="mem_kv_proj",
    )(mem, w_kv)


def _mix_xattn_body(att_ref, gdn_ref, x_ref, wmix_ref, wq_ref, k_ref, v_ref, wo_ref,
                    g0_ref, b0_ref, g1_ref, b1_ref, o_ref):
    n_sub = ROW_TILE // SUB_ROWS
    rows = [slice(r * SUB_ROWS, (r + 1) * SUB_ROWS) for r in range(n_sub)]
    head_cols = [slice(h * X_HEAD_DIM, (h + 1) * X_HEAD_DIM) for h in range(X_HEADS)]

    def mix(r):
        y = (_dot(att_ref[rows[r], :], wmix_ref[0:ATT_Q, :])
             + _dot(gdn_ref[rows[r], :], wmix_ref[ATT_Q:ATT_Q + GDN_W, :]))
        return _layer_norm(ALPHA * x_ref[rows[r], :] + y, g0_ref[...], b0_ref[...])

    def project_q(x1):
        return (_dot(x1.astype(BF16), wq_ref[...]) * X_SCALE).astype(BF16)

    def scores(q):
        return [_dot_nt(q[:, c], k_ref[:, c]) for c in head_cols]

    def softmax(s_heads):
        out = []
        for s in s_heads:
            p = jnp.exp(s - jnp.max(s, axis=-1, keepdims=True))
            out.append((p * (1.0 / jnp.sum(p, axis=-1, keepdims=True))).astype(BF16))
        return out

    def attend(p_heads):
        return jnp.concatenate([_dot(p, v_ref[:, c]).astype(BF16) for p, c in zip(p_heads, head_cols)], axis=1)

    def finish(r, x1, heads):
        y = _dot(heads, wo_ref[...])
        o_ref[rows[r], :] = _layer_norm(ALPHA * x1 + y, g1_ref[...], b1_ref[...])

    x1 = [mix(r) for r in range(n_sub)]
    s_prev = scores(project_q(x1[0]))
    for r in range(n_sub):
        q_next = project_q(x1[r + 1]) if r + 1 < n_sub else None
        heads = attend(softmax(s_prev))
        if q_next is not None:
            s_prev = scores(q_next)
        finish(r, x1[r], heads)


def _mix_xattn(att, gdn, x, w_mix, wq, k_mem, v_mem, wo, ln0, ln1, layer, batch, seq):
    t = x.shape[0]
    steps = seq // ROW_TILE
    mem_len = k_mem.shape[0] // batch
    row = lambda n: pl.BlockSpec((ROW_TILE, n), lambda bb, i: (bb * steps + i, 0))
    mem = pl.BlockSpec((mem_len, D_MODEL), lambda bb, i: (bb, 0))
    weight = _layer_weight(layer, (D_MODEL, D_MODEL))
    vec = _resident((1, D_MODEL))
    return pl.pallas_call(
        _mix_xattn_body,
        grid=(batch, steps),
        in_specs=[row(ATT_Q), row(GDN_W), row(D_MODEL), weight, weight, mem, mem, weight, vec, vec, vec, vec],
        out_specs=row(D_MODEL),
        out_shape=jax.ShapeDtypeStruct((t, D_MODEL), F32),
        compiler_params=_cparams("parallel", "parallel"),
        name="mix_xattn_ln",
    )(att, gdn, x, w_mix, wq, k_mem, v_mem, wo, *ln0, *ln1)


def _mlp_ln_body(x_ref, w1_ref, w2_ref, g_ref, b_ref, o_ref):
    x = x_ref[...]
    xb = x.astype(BF16)
    n_chunks = D_FF // FF_CHUNK

    def hidden(c):
        cols = slice(c * FF_CHUNK, (c + 1) * FF_CHUNK)
        return jnp.square(jnp.maximum(_dot(xb, w1_ref[:, cols]), 0.0)).astype(BF16), cols

    y = None
    for c in range(n_chunks - 1):
        hid, cols = hidden(c)
        part = _dot(hid, w2_ref[cols, :])
        y = part if y is None else y + part
    hid, cols = hidden(n_chunks - 1)
    for r in range(ROW_TILE // SUB_ROWS):
        rows = slice(r * SUB_ROWS, (r + 1) * SUB_ROWS)
        y_r = y[rows, :] + _dot(hid[rows, :], w2_ref[cols, :])
        o_ref[rows, :] = _layer_norm(ALPHA * x[rows, :] + y_r, g_ref[...], b_ref[...])


def _mlp_ln(x, w1, w2, g, b, layer):
    t = x.shape[0]
    row = pl.BlockSpec((ROW_TILE, D_MODEL), lambda i: (i, 0))
    return pl.pallas_call(
        _mlp_ln_body,
        grid=(t // ROW_TILE,),
        in_specs=[row, _layer_weight(layer, (D_MODEL, D_FF)), _layer_weight(layer, (D_FF, D_MODEL)),
                  _resident((1, D_MODEL)), _resident((1, D_MODEL))],
        out_specs=row,
        out_shape=jax.ShapeDtypeStruct((t, D_MODEL), F32),
        compiler_params=_cparams("parallel"),
        name="mlp_ln",
    )(x, w1, w2, g, b)


def _pack_w_in(w_in):
    w = w_in.astype(BF16)
    o = 0
    aq = w[..., o:o + ATT_Q]; o += ATT_Q
    ak = w[..., o:o + ATT_KV]; o += ATT_KV
    av = w[..., o:o + ATT_KV]; o += ATT_KV
    gqkv = w[..., o:o + 3 * GDN_W]; o += 3 * GDN_W
    ga = w[..., o:o + GDN_HEADS]; o += GDN_HEADS
    gb = w[..., o:o + GDN_HEADS]; o += GDN_HEADS
    gz = w[..., o:o + GDN_W]
    aq = jnp.concatenate([aq[..., h * ATT_HEAD_DIM:(h + 1) * ATT_HEAD_DIM] for h in ATT_HEAD_ORDER], axis=-1)
    pad = jnp.zeros(w.shape[:-1] + (LANES - 2 * GDN_HEADS,), BF16)
    return jnp.concatenate([aq, ak, av, gqkv, gz, ga, gb, pad], axis=-1)


def _pack_w_mix(w_mix_out):
    w = w_mix_out.astype(BF16)
    att = jnp.concatenate([w[:, h * ATT_HEAD_DIM:(h + 1) * ATT_HEAD_DIM, :] for h in ATT_HEAD_ORDER], axis=1)
    return jnp.concatenate([att, w[:, ATT_Q:, :]], axis=1)


def _pad_lanes(v):
    return jnp.pad(v, (0, LANES - v.shape[0]))


def kernel(x, mem, w_in, conv_w, attn_sinks, a_log, dt_bias, gdn_norm_g, w_mix_out,
           wq_mem, wk_mem, wv_mem, wo_mem, w_ff1, w_ff2, ln_g, ln_b):
    batch, seq, d_model = x.shape
    assert d_model == D_MODEL and mem.shape[0] == batch and mem.shape[2] == D_MODEL
    assert w_in.shape == (DEPTH, D_MODEL, ATT_Q + 2 * ATT_KV + 4 * GDN_W + 2 * GDN_HEADS)
    assert seq % ROW_TILE == 0 and seq % ATT_ROWS == 0 and seq % (GDN_STEP_CHUNKS * GDN_CHUNK) == 0
    assert (batch * seq) % IN_ROWS == 0
    assert (batch * mem.shape[1]) % min(ROW_TILE, batch * mem.shape[1]) == 0
    xt = x.reshape(batch * seq, D_MODEL)
    memt = mem.reshape(batch * mem.shape[1], D_MODEL)
    w_all = _pack_w_in(w_in)
    w_mix, wq, wo = _pack_w_mix(w_mix_out), wq_mem.astype(BF16), wo_mem.astype(BF16)
    w_kv = jnp.concatenate([wk_mem.astype(BF16), wv_mem.astype(BF16)], axis=-1)
    w1, w2 = w_ff1.astype(BF16), w_ff2.astype(BF16)
    for l in range(DEPTH):
        gate_params = jnp.zeros((SUBLANES, LANES), F32)
        gate_params = gate_params.at[0].set(_pad_lanes(a_log[l])).at[1].set(_pad_lanes(dt_bias[l]))
        ln = lambda i: (ln_g[l, i].reshape(1, D_MODEL), ln_b[l, i].reshape(1, D_MODEL))

        aq, akv, gqkv, gz, gab = _inproj(xt, w_all, l)
        att = _swa(aq, akv, attn_sinks[l], batch, seq)
        gdn = _gdn(gqkv, gab, gz, conv_w[l], gate_params, gdn_norm_g[l].reshape(1, GDN_HEAD_DIM), batch, seq)
        k_mem, v_mem = _memkv(memt, w_kv, l)
        xt = _mix_xattn(att, gdn, xt, w_mix, wq, k_mem, v_mem, wo, ln(0), ln(1), l, batch, seq)
        xt = _mlp_ln(xt, w1, w2, *ln(2), l)
    return xt.reshape(batch, seq, D_MODEL)
```

```python
import functools

import jax
import jax.numpy as jnp
from jax import lax
from jax.experimental import pallas as pl
from jax.experimental.pallas import tpu as pltpu

F32 = jnp.float32
BF16 = jnp.bfloat16

D_MODEL = 1024
DEPTH = 2
ATT_HEADS = 8
ATT_KV_HEADS = 2
ATT_HEAD_DIM = 64
ATT_GROUP = ATT_HEADS // ATT_KV_HEADS
WINDOW = 128
ATT_Q = ATT_HEADS * ATT_HEAD_DIM
ATT_KV = ATT_KV_HEADS * ATT_HEAD_DIM
GDN_HEADS = 4
GDN_HEAD_DIM = 128
GDN_W = GDN_HEADS * GDN_HEAD_DIM
CONV_K = 4
X_HEADS = 4
X_HEAD_DIM = D_MODEL // X_HEADS
D_FF = 4 * D_MODEL
ALPHA = (2 * DEPTH) ** 0.25
LN_EPS = 1e-5
RMS_EPS = 1e-6
NEG_INF = -1e30
ATT_SCALE = ATT_HEAD_DIM ** -0.5
X_SCALE = X_HEAD_DIM ** -0.5
ALIBI_SLOPES = tuple(2.0 ** (-8.0 * (h + 1) / ATT_HEADS) for h in range(ATT_HEADS))
ATT_HEAD_ORDER = tuple(h for p in range(ATT_GROUP) for h in (p, p + ATT_GROUP))

LANES = 128
SUBLANES = 8
ROW_TILE = 512
IN_ROWS = 1024
SUB_ROWS = 256
ATT_ROWS = 1024
GDN_CHUNK = 128
GDN_STEP_CHUNKS = 4
FF_CHUNK = 2048
VMEM_LIMIT = 56 * 1024 * 1024

_C_AQ = (0, ATT_Q)
_C_AKV = (_C_AQ[1], _C_AQ[1] + 2 * LANES)
_C_GQKV = (_C_AKV[1], _C_AKV[1] + 3 * GDN_W)
_C_GZ = (_C_GQKV[1], _C_GQKV[1] + GDN_W)
_C_GAB = (_C_GZ[1], _C_GZ[1] + LANES)
N_IN = _C_GAB[1]


def _cparams(*sem):
    return pltpu.CompilerParams(dimension_semantics=sem, vmem_limit_bytes=VMEM_LIMIT)


def _resident(shape):
    nd = len(shape)
    return pl.BlockSpec(shape, lambda *_: (0,) * nd, pipeline_mode=pl.Buffered(1))


def _layer_weight(layer, shape):
    nd = len(shape)
    return pl.BlockSpec((None,) + tuple(shape), lambda *_: (layer,) + (0,) * nd, pipeline_mode=pl.Buffered(1))


def _dot(a, b):
    return jnp.dot(a, b, preferred_element_type=F32)


def _dot_nt(a, b):
    return lax.dot_general(a, b, (((1,), (1,)), ((), ())), preferred_element_type=F32)


def _layer_norm(xf, g, b):
    mu = jnp.mean(xf, axis=-1, keepdims=True)
    d = xf - mu
    var = jnp.mean(d * d, axis=-1, keepdims=True)
    return d * lax.rsqrt(var + LN_EPS) * g + b


def _silu(x):
    h = 0.5 * x
    return h + h * jnp.tanh(h)


def _inproj_body(x_ref, w_ref, aq_ref, akv_ref, gqkv_ref, gz_ref, gab_ref):
    xb = x_ref[...].astype(BF16)

    def mm(lo, hi):
        return _dot(xb, w_ref[:, lo:hi])

    aq_ref[...] = mm(*_C_AQ).astype(BF16)
    akv_ref[...] = mm(*_C_AKV).astype(BF16)
    for c in range(3):
        lo = _C_GQKV[0] + c * GDN_W
        gqkv_ref[:, c * GDN_W:(c + 1) * GDN_W] = mm(lo, lo + GDN_W)
    gz_ref[...] = mm(*_C_GZ)
    gab_ref[...] = mm(*_C_GAB)


def _inproj(x, w_all, layer):
    t = x.shape[0]
    row = lambda n: pl.BlockSpec((IN_ROWS, n), lambda i: (i, 0))
    return pl.pallas_call(
        _inproj_body,
        grid=(t // IN_ROWS,),
        in_specs=[row(D_MODEL), _layer_weight(layer, (D_MODEL, N_IN))],
        out_specs=[row(ATT_Q), row(2 * LANES), row(3 * GDN_W), row(GDN_W), row(LANES)],
        out_shape=[jax.ShapeDtypeStruct((t, ATT_Q), BF16),
                   jax.ShapeDtypeStruct((t, 2 * LANES), BF16),
                   jax.ShapeDtypeStruct((t, 3 * GDN_W), F32),
                   jax.ShapeDtypeStruct((t, GDN_W), F32),
                   jax.ShapeDtypeStruct((t, LANES), F32)],
        compiler_params=_cparams("parallel"),
        name="in_proj",
    )(x, w_all)


def _swa_body(sinks_ref, q_ref, kvc_ref, kvp_ref, o_ref):
    first = pl.program_id(1) == 0
    lane = lax.broadcasted_iota(jnp.int32, (1, LANES), 1)
    q_masks = (jnp.where(lane < ATT_HEAD_DIM, ATT_SCALE, 0.0).astype(BF16),
               jnp.where(lane >= ATT_HEAD_DIM, ATT_SCALE, 0.0).astype(BF16))
    low_half = lax.broadcasted_iota(jnp.int32, (WINDOW, LANES), 1) < ATT_HEAD_DIM
    qi = lax.broadcasted_iota(jnp.int32, (WINDOW, WINDOW), 0)
    kj = lax.broadcasted_iota(jnp.int32, (WINDOW, WINDOW), 1)
    in_cur = kj <= qi
    distf = jnp.where(in_cur, qi - kj, qi + WINDOW - kj).astype(F32)
    visible_first = in_cur | (kj >= jnp.where(first, WINDOW, 0))

    n_blk = ATT_ROWS // WINDOW
    rows = [slice(blk * WINDOW, (blk + 1) * WINDOW) for blk in range(n_blk)]

    def band(blk):
        prev = kvp_ref[...] if blk == 0 else kvc_ref[rows[blk - 1], :]
        return jnp.concatenate([prev, kvc_ref[rows[blk], :]], axis=0)

    def scores(blk):
        kk = band(blk)[:, 0:LANES]
        out = []
        for pos in range(ATT_HEADS):
            qp = q_ref[rows[blk], (pos // 2) * LANES:(pos // 2 + 1) * LANES]
            out.append(_dot_nt(qp * q_masks[pos % 2], kk))
        return out

    def attend(blk, s2_heads):
        vv = band(blk)[:, LANES:2 * LANES]
        halves = []
        for pos, s2 in enumerate(s2_heads):
            hq = ATT_HEAD_ORDER[pos]
            s = jnp.where(in_cur, s2[:, WINDOW:], s2[:, :WINDOW]) - ALIBI_SLOPES[hq] * distf
            if blk == 0:
                s = jnp.where(visible_first, s, NEG_INF)
            sink = sinks_ref[hq]
            m = jnp.maximum(jnp.max(s, axis=-1, keepdims=True), sink)
            p = jnp.exp(s - m)
            denom = jnp.sum(p, axis=-1, keepdims=True) + jnp.exp(sink - m)
            pn = p * (1.0 / denom)
            p2 = jnp.concatenate([jnp.where(in_cur, 0.0, pn), jnp.where(in_cur, pn, 0.0)], axis=1)
            halves.append(_dot(p2.astype(BF16), vv))
        for pair in range(ATT_HEADS // 2):
            o_ref[rows[blk], pair * LANES:(pair + 1) * LANES] = jnp.where(
                low_half, halves[2 * pair], halves[2 * pair + 1]).astype(BF16)

    s2_heads = scores(0)
    for blk in range(n_blk):
        s2_next = scores(blk + 1) if blk + 1 < n_blk else None
        attend(blk, s2_heads)
        s2_heads = s2_next


def _swa(aq, akv, sinks, batch, seq):
    t = aq.shape[0]
    steps = seq // ATT_ROWS
    blocks_per_step = ATT_ROWS // WINDOW
    blocks_per_seq = seq // WINDOW

    def prev_map(b, j):
        return (b * blocks_per_seq + jnp.maximum(j * blocks_per_step - 1, 0), 0)

    return pl.pallas_call(
        _swa_body,
        grid=(batch, steps),
        in_specs=[pl.BlockSpec(memory_space=pltpu.SMEM),
                  pl.BlockSpec((ATT_ROWS, ATT_Q), lambda b, j: (b * steps + j, 0)),
                  pl.BlockSpec((ATT_ROWS, 2 * LANES), lambda b, j: (b * steps + j, 0)),
                  pl.BlockSpec((WINDOW, 2 * LANES), prev_map)],
        out_specs=pl.BlockSpec((ATT_ROWS, ATT_Q), lambda b, j: (b * steps + j, 0)),
        out_shape=jax.ShapeDtypeStruct((t, ATT_Q), BF16),
        compiler_params=_cparams("parallel", "parallel"),
        name="swa_attention",
    )(sinks, aq, akv, akv)


def _unit_lower_inverse_stages(mats, ii, jj, out):
    base = 16

    def same_block(size):
        return (ii // size) == (jj // size)

    def bf(ms):
        return [m.astype(BF16) for m in ms]

    eye = (ii == jj).astype(F32)
    in_base = same_block(base)
    b1 = [jnp.where(in_base, -a, 0.0) for a in mats]
    b1_b = bf(b1)
    b2_b = bf([_dot(b, b) for b in b1_b])
    yield
    t = [eye + b for b in b1]
    b4 = [_dot(b, b) for b in b2_b]
    t = [x + _dot(xb, p) for x, xb, p in zip(t, bf(t), b2_b)]
    b4_b = bf(b4)
    yield
    b8 = [_dot(b, b) for b in b4_b]
    t = [x + _dot(xb, p) for x, xb, p in zip(t, bf(t), b4_b)]
    b8_b = bf(b8)
    yield
    t = [x + _dot(xb, p) for x, xb, p in zip(t, bf(t), b8_b)]
    yield
    size = base
    while size < GDN_CHUNK:
        off_diag = same_block(2 * size) & jnp.logical_not(same_block(size))
        t_b = bf(t)
        ct_b = bf([_dot(jnp.where(off_diag, a, 0.0).astype(BF16), xb) for a, xb in zip(mats, t_b)])
        yield
        t = [x - _dot(xb, y) for x, xb, y in zip(t, t_b, ct_b)]
        yield
        size *= 2
    out.extend(t)


def _gdn_body(gp_ref, convw_ref, normg_ref, x_ref, xprev_ref, gab_ref, z_ref, o_ref,
              xs_ref, a_ref, rhs_ref, qd_ref, kdt_ref, intra_ref, cd_ref, state_ref, qkv_ref,
              *, groups_per_seq, n_groups):
    step = pl.program_id(0)
    c_rows = GDN_CHUNK
    subs = range(GDN_STEP_CHUNKS)
    heads = range(GDN_HEADS)
    pairs = [(j, h) for j in subs for h in heads]
    slot = lambda j, h: j * GDN_HEADS + h
    head_cols = [slice(h * GDN_HEAD_DIM, (h + 1) * GDN_HEAD_DIM) for h in heads]
    chunk_rows = [slice(j * c_rows, (j + 1) * c_rows) for j in subs]

    @pl.when(step == 0)
    def _():
        for ref in (a_ref, rhs_ref, qd_ref, kdt_ref, intra_ref, cd_ref, state_ref):
            ref[...] = jnp.zeros_like(ref)

    ii = lax.broadcasted_iota(jnp.int32, (c_rows, c_rows), 0)
    jj = lax.broadcasted_iota(jnp.int32, (c_rows, c_rows), 1)
    causal = jj <= ii
    strict = jj < ii

    def finish():
        t_inv = []
        yield from _unit_lower_inverse_stages([a_ref[slot(j, h)] for j, h in pairs], ii, jj, t_inv)
        uw = [_dot(t.astype(BF16), rhs_ref[slot(j, h)]) for (j, h), t in zip(pairs, t_inv)]
        yield
        keep = jnp.where(lax.rem(step - 1, groups_per_seq) == 0, 0.0, 1.0)
        state = [state_ref[h] * keep for h in heads]
        for j in subs:
            r = [_dot(jnp.concatenate([uw[slot(j, h)][:, GDN_HEAD_DIM:].astype(BF16),
                                       qd_ref[j, :, head_cols[h]]], axis=0),
                      state[h].astype(BF16)) for h in heads]
            yield
            v_new_b = [(uw[slot(j, h)][:, :GDN_HEAD_DIM] - r[h][:c_rows]).astype(BF16) for h in heads]
            state = [state[h] * cd_ref[j, 0:1, h:h + 1] + _dot(kdt_ref[j, :, head_cols[h]], v_new_b[h])
                     for h in heads]
            yield
            for h in heads:
                o = r[h][c_rows:] + _dot(intra_ref[j, :, head_cols[h]], v_new_b[h])
                o = (o * lax.rsqrt(jnp.mean(o * o, axis=-1, keepdims=True) + RMS_EPS)
                     * normg_ref[...] * _silu(z_ref[chunk_rows[j], head_cols[h]]))
                o_ref[chunk_rows[j], head_cols[h]] = o.astype(BF16)
            yield
        for h in heads:
            state_ref[h] = state[h]

    first = lax.rem(jnp.minimum(step, n_groups - 1), groups_per_seq) == 0
    xs_ref[0:SUBLANES, :] = jnp.where(first, 0.0, xprev_ref[...])
    xs_ref[SUBLANES:SUBLANES + GDN_STEP_CHUNKS * c_rows, :] = x_ref[...]

    def convolve():
        for j, h in pairs:
            for part in range(3):
                col = part * GDN_W + h * GDN_HEAD_DIM
                window = xs_ref[j * c_rows:j * c_rows + SUBLANES + c_rows, col:col + LANES]
                w0, w1, w2, w3 = (convw_ref[tap:tap + 1, col:col + LANES] for tap in range(CONV_K))
                x1 = pltpu.roll(window, 1, axis=0)
                u = w1 * window + w0 * x1
                acc = (w3 * window + w2 * x1 + pltpu.roll(u, 2, axis=0))[SUBLANES:, :]
                y = _silu(acc)
                if part < 2:
                    y = y * lax.rsqrt(jnp.sum(y * y, axis=-1, keepdims=True) + RMS_EPS)
                qkv_ref[3 * slot(j, h) + part] = y
                yield

    finishing, convolving = finish(), convolve()
    n_finish, n_conv = 11 + 3 * GDN_STEP_CHUNKS, 3 * len(pairs)
    done = 0
    for turn, _ in enumerate(finishing):
        while done * n_finish < (turn + 1) * n_conv and next(convolving, True) is None:
            done += 1
    for _ in convolving:
        pass

    a_log = gp_ref[0:1, :]
    dt_bias = gp_ref[1:2, :]
    ltri = causal.astype(F32)
    for j in subs:
        gab = gab_ref[chunk_rows[j], :]
        g_all = -jnp.exp(a_log) * jax.nn.softplus(gab + dt_bias)
        beta_all = jax.nn.sigmoid(gab)
        decay_cols = jnp.dot(ltri, g_all, precision=lax.Precision.HIGHEST, preferred_element_type=F32)
        decay_rows = lax.dot_general(g_all.T[0:SUBLANES, :], ltri, (((1,), (1,)), ((), ())),
                                     precision=lax.Precision.HIGHEST, preferred_element_type=F32)
        decay_last = decay_cols[c_rows - 1:c_rows, :]
        cd_ref[j] = jnp.broadcast_to(jnp.exp(decay_last), (SUBLANES, LANES))
        for h in heads:
            cols = head_cols[h]
            qn = qkv_ref[3 * slot(j, h)] * (GDN_HEAD_DIM ** -0.5)
            kn = qkv_ref[3 * slot(j, h) + 1]
            v = qkv_ref[3 * slot(j, h) + 2]
            dcol = decay_cols[:, h:h + 1]
            drow = decay_rows[h:h + 1, :]
            lmask = jnp.where(causal, jnp.exp(jnp.where(causal, dcol - drow, 0.0)), 0.0)
            e_dec = jnp.exp(dcol)
            e_rem = jnp.exp(decay_last[:, h:h + 1] - dcol)
            beta = beta_all[:, GDN_HEADS + h:GDN_HEADS + h + 1]
            kbeta = kn * beta
            kn_b = kn.astype(BF16)
            a_ref[slot(j, h)] = _dot_nt(kbeta.astype(BF16), kn_b) * jnp.where(strict, lmask, 0.0)
            intra_ref[j, :, cols] = (_dot_nt(qn.astype(BF16), kn_b) * lmask).astype(BF16)
            rhs_ref[slot(j, h)] = jnp.concatenate([v * beta, kbeta * e_dec], axis=1).astype(BF16)
            qd_ref[j, :, cols] = (qn * e_dec).astype(BF16)
            kdt_ref[j, :, cols] = (kn * e_rem).T.astype(BF16)


def _gdn(gqkv, gab, gz, conv_w, gate_params, norm_g, batch, seq):
    t = gqkv.shape[0]
    group = GDN_STEP_CHUNKS * GDN_CHUNK
    groups_per_seq = seq // group
    n_groups = batch * groups_per_seq
    per8 = group // SUBLANES
    n_slots = GDN_STEP_CHUNKS * GDN_HEADS
    cur = lambda s: jnp.minimum(s, n_groups - 1)
    done = lambda s: jnp.maximum(s - 1, 0)
    vmem = lambda shape, dt: pltpu.VMEM(shape, dt)
    return pl.pallas_call(
        functools.partial(_gdn_body, groups_per_seq=groups_per_seq, n_groups=n_groups),
        grid=(n_groups + 1,),
        in_specs=[_resident((SUBLANES, LANES)), _resident((CONV_K, 3 * GDN_W)),
                  _resident((1, GDN_HEAD_DIM)),
                  pl.BlockSpec((group, 3 * GDN_W), lambda s: (cur(s), 0)),
                  pl.BlockSpec((SUBLANES, 3 * GDN_W), lambda s: (jnp.maximum(cur(s) * per8 - 1, 0), 0)),
                  pl.BlockSpec((group, LANES), lambda s: (cur(s), 0)),
                  pl.BlockSpec((group, GDN_W), lambda s: (done(s), 0))],
        out_specs=pl.BlockSpec((group, GDN_W), lambda s: (done(s), 0)),
        out_shape=jax.ShapeDtypeStruct((t, GDN_W), BF16),
        scratch_shapes=[vmem((SUBLANES + group, 3 * GDN_W), F32),
                        vmem((n_slots, GDN_CHUNK, GDN_CHUNK), F32),
                        vmem((n_slots, GDN_CHUNK, 2 * GDN_HEAD_DIM), BF16),
                        vmem((GDN_STEP_CHUNKS, GDN_CHUNK, GDN_W), BF16),
                        vmem((GDN_STEP_CHUNKS, GDN_HEAD_DIM, GDN_HEADS * GDN_CHUNK), BF16),
                        vmem((GDN_STEP_CHUNKS, GDN_CHUNK, GDN_HEADS * GDN_CHUNK), BF16),
                        vmem((GDN_STEP_CHUNKS, SUBLANES, LANES), F32),
                        vmem((GDN_HEADS, GDN_HEAD_DIM, GDN_HEAD_DIM), F32),
                        vmem((3 * n_slots, GDN_CHUNK, GDN_HEAD_DIM), F32)],
        compiler_params=_cparams("arbitrary"),
        name="gdn",
    )(gate_params, conv_w, norm_g, gqkv, gqkv, gab, gz)


def _memkv_body(m_ref, w_ref, k_ref, v_ref):
    mb = m_ref[...].astype(BF16)
    k_ref[...] = _dot(mb, w_ref[:, 0:D_MODEL]).astype(BF16)
    v_ref[...] = _dot(mb, w_ref[:, D_MODEL:2 * D_MODEL]).astype(BF16)


def _memkv(mem, w_kv, layer):
    rows = mem.shape[0]
    tile = min(ROW_TILE, rows)
    row = pl.BlockSpec((tile, D_MODEL), lambda i: (i, 0))
    return pl.pallas_call(
        _memkv_body,
        grid=(rows // tile,),
        in_specs=[row, _layer_weight(layer, (D_MODEL, 2 * D_MODEL))],
        out_specs=[row, row],
        out_shape=[jax.ShapeDtypeStruct((rows, D_MODEL), BF16)] * 2,
        compiler_params=_cparams("parallel"),
        name="mem_kv_proj",
    )(mem, w_kv)


def _mix_xattn_body(att_ref, gdn_ref, x_ref, wmix_ref, wq_ref, k_ref, v_ref, wo_ref,
                    g0_ref, b0_ref, g1_ref, b1_ref, o_ref):
    n_sub = ROW_TILE // SUB_ROWS
    rows = [slice(r * SUB_ROWS, (r + 1) * SUB_ROWS) for r in range(n_sub)]
    head_cols = [slice(h * X_HEAD_DIM, (h + 1) * X_HEAD_DIM) for h in range(X_HEADS)]

    def mix(r):
        y = (_dot(att_ref[rows[r], :], wmix_ref[0:ATT_Q, :])
             + _dot(gdn_ref[rows[r], :], wmix_ref[ATT_Q:ATT_Q + GDN_W, :]))
        return _layer_norm(ALPHA * x_ref[rows[r], :] + y, g0_ref[...], b0_ref[...])

    def project_q(x1):
        return (_dot(x1.astype(BF16), wq_ref[...]) * X_SCALE).astype(BF16)

    def scores(q):
        return [_dot_nt(q[:, c], k_ref[:, c]) for c in head_cols]

    def softmax(s_heads):
        out = []
        for s in s_heads:
            p = jnp.exp(s - jnp.max(s, axis=-1, keepdims=True))
            out.append((p * (1.0 / jnp.sum(p, axis=-1, keepdims=True))).astype(BF16))
        return out

    def attend(p_heads):
        return jnp.concatenate([_dot(p, v_ref[:, c]).astype(BF16) for p, c in zip(p_heads, head_cols)], axis=1)

    def finish(r, x1, heads):
        y = _dot(heads, wo_ref[...])
        o_ref[rows[r], :] = _layer_norm(ALPHA * x1 + y, g1_ref[...], b1_ref[...])

    x1 = [mix(r) for r in range(n_sub)]
    s_prev = scores(project_q(x1[0]))
    for r in range(n_sub):
        q_next = project_q(x1[r + 1]) if r + 1 < n_sub else None
        heads = attend(softmax(s_prev))
        if q_next is not None:
            s_prev = scores(q_next)
        finish(r, x1[r], heads)


def _mix_xattn(att, gdn, x, w_mix, wq, k_mem, v_mem, wo, ln0, ln1, layer, batch, seq):
    t = x.shape[0]
    steps = seq // ROW_TILE
    mem_len = k_mem.shape[0] // batch
    row = lambda n: pl.BlockSpec((ROW_TILE, n), lambda bb, i: (bb * steps + i, 0))
    mem = pl.BlockSpec((mem_len, D_MODEL), lambda bb, i: (bb, 0))
    weight = _layer_weight(layer, (D_MODEL, D_MODEL))
    vec = _resident((1, D_MODEL))
    return pl.pallas_call(
        _mix_xattn_body,
        grid=(batch, steps),
        in_specs=[row(ATT_Q), row(GDN_W), row(D_MODEL), weight, weight, mem, mem, weight, vec, vec, vec, vec],
        out_specs=row(D_MODEL),
        out_shape=jax.ShapeDtypeStruct((t, D_MODEL), F32),
        compiler_params=_cparams("parallel", "parallel"),
        name="mix_xattn_ln",
    )(att, gdn, x, w_mix, wq, k_mem, v_mem, wo, *ln0, *ln1)


def _mlp_ln_body(x_ref, w1_ref, w2_ref, g_ref, b_ref, o_ref):
    x = x_ref[...]
    xb = x.astype(BF16)
    n_chunks = D_FF // FF_CHUNK

    def hidden(c):
        cols = slice(c * FF_CHUNK, (c + 1) * FF_CHUNK)
        return jnp.square(jnp.maximum(_dot(xb, w1_ref[:, cols]), 0.0)).astype(BF16), cols

    y = None
    for c in range(n_chunks - 1):
        hid, cols = hidden(c)
        part = _dot(hid, w2_ref[cols, :])
        y = part if y is None else y + part
    hid, cols = hidden(n_chunks - 1)
    for r in range(ROW_TILE // SUB_ROWS):
        rows = slice(r * SUB_ROWS, (r + 1) * SUB_ROWS)
        y_r = y[rows, :] + _dot(hid[rows, :], w2_ref[cols, :])
        o_ref[rows, :] = _layer_norm(ALPHA * x[rows, :] + y_r, g_ref[...], b_ref[...])


def _mlp_ln(x, w1, w2, g, b, layer):
    t = x.shape[0]
    row = pl.BlockSpec((ROW_TILE, D_MODEL), lambda i: (i, 0))
    return pl.pallas_call(
        _mlp_ln_body,
        grid=(t // ROW_TILE,),
        in_specs=[row, _layer_weight(layer, (D_MODEL, D_FF)), _layer_weight(layer, (D_FF, D_MODEL)),
                  _resident((1, D_MODEL)), _resident((1, D_MODEL))],
        out_specs=row,
        out_shape=jax.ShapeDtypeStruct((t, D_MODEL), F32),
        compiler_params=_cparams("parallel"),
        name="mlp_ln",
    )(x, w1, w2, g, b)


def _pack_w_in(w_in):
    w = w_in.astype(BF16)
    o = 0
    aq = w[..., o:o + ATT_Q]; o += ATT_Q
    ak = w[..., o:o + ATT_KV]; o += ATT_KV
    av = w[..., o:o + ATT_KV]; o += ATT_KV
    gqkv = w[..., o:o + 3 * GDN_W]; o += 3 * GDN_W
    ga = w[..., o:o + GDN_HEADS]; o += GDN_HEADS
    gb = w[..., o:o + GDN_HEADS]; o += GDN_HEADS
    gz = w[..., o:o + GDN_W]
    aq = jnp.concatenate([aq[..., h * ATT_HEAD_DIM:(h + 1) * ATT_HEAD_DIM] for h in ATT_HEAD_ORDER], axis=-1)
    pad = jnp.zeros(w.shape[:-1] + (LANES - 2 * GDN_HEADS,), BF16)
    return jnp.concatenate([aq, ak, av, gqkv, gz, ga, gb, pad], axis=-1)


def _pack_w_mix(w_mix_out):
    w = w_mix_out.astype(BF16)
    att = jnp.concatenate([w[:, h * ATT_HEAD_DIM:(h + 1) * ATT_HEAD_DIM, :] for h in ATT_HEAD_ORDER], axis=1)
    return jnp.concatenate([att, w[:, ATT_Q:, :]], axis=1)


def _pad_lanes(v):
    return jnp.pad(v, (0, LANES - v.shape[0]))


def kernel(x, mem, w_in, conv_w, attn_sinks, a_log, dt_bias, gdn_norm_g, w_mix_out,
           wq_mem, wk_mem, wv_mem, wo_mem, w_ff1, w_ff2, ln_g, ln_b):
    batch, seq, d_model = x.shape
    assert d_model == D_MODEL and mem.shape[0] == batch and mem.shape[2] == D_MODEL
    assert w_in.shape == (DEPTH, D_MODEL, ATT_Q + 2 * ATT_KV + 4 * GDN_W + 2 * GDN_HEADS)
    assert seq % ROW_TILE == 0 and seq % ATT_ROWS == 0 and seq % (GDN_STEP_CHUNKS * GDN_CHUNK) == 0
    assert (batch * seq) % IN_ROWS == 0
    assert (batch * mem.shape[1]) % min(ROW_TILE, batch * mem.shape[1]) == 0
    xt = x.reshape(batch * seq, D_MODEL)
    memt = mem.reshape(batch * mem.shape[1], D_MODEL)
    w_all = _pack_w_in(w_in)
    w_mix, wq, wo = _pack_w_mix(w_mix_out), wq_mem.astype(BF16), wo_mem.astype(BF16)
    w_kv = jnp.concatenate([wk_mem.astype(BF16), wv_mem.astype(BF16)], axis=-1)
    w1, w2 = w_ff1.astype(BF16), w_ff2.astype(BF16)
    for l in range(DEPTH):
        gate_params = jnp.zeros((SUBLANES, LANES), F32)
        gate_params = gate_params.at[0].set(_pad_lanes(a_log[l])).at[1].set(_pad_lanes(dt_bias[l]))
        ln = lambda i: (ln_g[l, i].reshape(1, D_MODEL), ln_b[l, i].reshape(1, D_MODEL))

        aq, akv, gqkv, gz, gab = _inproj(xt, w_all, l)
        att = _swa(aq, akv, attn_sinks[l], batch, seq)
        gdn = _gdn(gqkv, gab, gz, conv_w[l], gate_params, gdn_norm_g[l].reshape(1, GDN_HEAD_DIM), batch, seq)
        k_mem, v_mem = _memkv(memt, w_kv, l)
        xt = _mix_xattn(att, gdn, xt, w_mix, wq, k_mem, v_mem, wo, ln(0), ln(1), l, batch, seq)
        xt = _mlp_ln(xt, w1, w2, *ln(2), l)
    return xt.reshape(batch, seq, D_MODEL)
```
